```python
import math
import jax, jax.numpy as jnp
from jax import lax
import numpy as np

D_MODEL = 2048
BATCH = 16
SEQ = 2048
DEPTH = 4

N_A = DEPTH // 2
N_B = DEPTH - N_A
A_QK_DIM = 128
A_V_DIM = 2 * A_QK_DIM
A_HEADS = D_MODEL // A_V_DIM
A_WIDTH = A_HEADS * A_V_DIM
B_HEAD_DIM = 128
B_HEADS = D_MODEL // B_HEAD_DIM
B_WIDTH = B_HEADS * B_HEAD_DIM
BLOCK = 128
NEG = -1e30
EPS = 1e-5
DN_ALPHA = (2.0 * DEPTH) ** 0.25
DN_BETA = (8.0 * DEPTH) ** -0.25

kernel_name = "yoco_diffattn_fox_hybrid"


def layer_norm(x, g, b):
    xf = x.astype(jnp.float32)
    mu = jnp.mean(xf, axis=-1, keepdims=True)
    var = jnp.mean(jnp.square(xf - mu), axis=-1, keepdims=True)
    y = (xf - mu) * lax.rsqrt(var + EPS)
    return (y * g.astype(jnp.float32) + b.astype(jnp.float32)).astype(x.dtype)


def rms_norm(x, g):
    xf = x.astype(jnp.float32)
    y = xf * lax.rsqrt(jnp.mean(jnp.square(xf), axis=-1, keepdims=True) + EPS)
    return (y * g.astype(jnp.float32)).astype(x.dtype)


def to_blocks(a):
    b, s = a.shape[:2]
    a = a.reshape((b, s // BLOCK, BLOCK) + a.shape[2:])
    return jnp.moveaxis(a, 1, 0)


def from_blocks(a):
    a = jnp.moveaxis(a, 0, 1)
    return a.reshape((a.shape[0], a.shape[1] * a.shape[2]) + a.shape[3:])


def alibi_slopes(n_heads):
    return jnp.asarray([2.0 ** (-8.0 * (h + 1) / n_heads) for h in range(n_heads)], dtype=jnp.float32)


def diff_attention(h, w_in, w_out, lq1, lk1, lq2, lk2, subln_g, layer_idx):
    B, S, _ = h.shape
    proj = h @ w_in
    q, k, v, g = jnp.split(proj, 4, axis=-1)
    q = q.reshape(B, S, A_HEADS, 2, A_QK_DIM) * (A_QK_DIM ** -0.5)
    k = k.reshape(B, S, A_HEADS, 2, A_QK_DIM)
    v = v.reshape(B, S, A_HEADS, A_V_DIM)
    lam_init = 0.8 - 0.6 * math.exp(-0.3 * layer_idx)
    f32 = jnp.float32
    lam = (jnp.exp(jnp.sum(lq1.astype(f32) * lk1.astype(f32)))
           - jnp.exp(jnp.sum(lq2.astype(f32) * lk2.astype(f32))) + lam_init)
    slopes = alibi_slopes(A_HEADS)
    s_pos = jnp.arange(S)

    def block(args):
        qb, i = args
        t_pos = i * BLOCK + jnp.arange(BLOCK)
        dist = (t_pos[:, None] - s_pos[None, :]).astype(f32)
        bias = -slopes[:, None, None] * dist
        sc = jnp.einsum('bqhcd,bkhcd->bhcqk', qb, k).astype(f32) + bias[None, :, None]
        sc = jnp.where(dist >= 0, sc, NEG)
        p = jax.nn.softmax(sc, axis=-1)
        a = p[:, :, 0] - lam * p[:, :, 1]
        return jnp.einsum('bhqk,bkhe->bqhe', a.astype(v.dtype), v)

    o = from_blocks(lax.map(block, (to_blocks(q), jnp.arange(S // BLOCK))))
    o = rms_norm(o, subln_g) * (1.0 - lam_init)
    o = o.reshape(B, S, A_WIDTH) * jax.nn.silu(g)
    return o @ w_out


def shared_kv(x, c_act, w_mod_kv, b_mod_kv, w_kv, b_f):
    B, S, _ = x.shape
    mod = c_act @ w_mod_kv + b_mod_kv
    shift, scale = jnp.split(mod, 2, axis=-1)
    hk = x * (1.0 + scale[:, None]) + shift[:, None]
    kvf = hk @ w_kv
    k, v, zf = jnp.split(kvf, [B_WIDTH, 2 * B_WIDTH], axis=-1)
    k = k.reshape(B, S, B_HEADS, B_HEAD_DIM)
    v = v.reshape(B, S, B_HEADS, B_HEAD_DIM)
    log_f = jax.nn.log_sigmoid(zf.astype(jnp.float32) + b_f.astype(jnp.float32))
    F = jnp.cumsum(log_f, axis=1)
    return k, v, F


def forgetting_attention(h, w_q, w_out, k, v, F):
    B, S, _ = h.shape
    f32 = jnp.float32
    proj = h @ w_q
    q, g = jnp.split(proj, 2, axis=-1)
    q = q.reshape(B, S, B_HEADS, B_HEAD_DIM) * (B_HEAD_DIM ** -0.5)
    Fk = jnp.moveaxis(F, 1, 2)
    s_pos = jnp.arange(S)

    def block(args):
        qb, Fq, i = args
        t_pos = i * BLOCK + jnp.arange(BLOCK)
        causal = t_pos[:, None] >= s_pos[None, :]
        decay = jnp.moveaxis(Fq, 1, 2)[..., None] - Fk[:, :, None, :]
        sc = jnp.einsum('bqhd,bkhd->bhqk', qb, k).astype(f32) + decay
        sc = jnp.where(causal, sc, NEG)
        p = jax.nn.softmax(sc, axis=-1)
        return jnp.einsum('bhqk,bkhd->bqhd', p.astype(v.dtype), v)

    o = from_blocks(lax.map(block, (to_blocks(q), to_blocks(F), jnp.arange(S // BLOCK))))
    o = o.reshape(B, S, B_WIDTH) * jax.nn.silu(g)
    return o @ w_out


def setup_inputs(seed: int = 0) -> dict:
    key = jax.random.key(seed)
    ks = jax.random.split(key, 20)
    nrm = jax.random.normal
    D = D_MODEL
    x = nrm(ks[0], (BATCH, SEQ, D), jnp.float32)
    c = nrm(ks[1], (BATCH, D), jnp.float32)
    w_mod = nrm(ks[2], (DEPTH, D, 3 * D), jnp.float32) * D ** -0.5
    b_mod = nrm(ks[3], (DEPTH, 3 * D), jnp.float32) * 0.02
    ln_g = 1.0 + 0.02 * nrm(ks[4], (DEPTH, D), jnp.float32)
    ln_b = 0.02 * nrm(ks[5], (DEPTH, D), jnp.float32)
    a_w_in = nrm(ks[6], (N_A, D, 4 * A_WIDTH), jnp.float32) * D ** -0.5
    a_w_out = nrm(ks[7], (N_A, A_WIDTH, D), jnp.float32) * (A_WIDTH ** -0.5 * DN_BETA)
    a_lam_q1 = 0.1 * nrm(ks[8], (N_A, A_QK_DIM), jnp.float32)
    a_lam_k1 = 0.1 * nrm(ks[9], (N_A, A_QK_DIM), jnp.float32)
    a_lam_q2 = 0.1 * nrm(ks[10], (N_A, A_QK_DIM), jnp.float32)
    a_lam_k2 = 0.1 * nrm(ks[11], (N_A, A_QK_DIM), jnp.float32)
    a_subln_g = 1.0 + 0.02 * nrm(ks[12], (N_A, A_V_DIM), jnp.float32)
    kv_w_mod = nrm(ks[13], (D, 2 * D), jnp.float32) * D ** -0.5
    kv_b_mod = nrm(ks[14], (2 * D,), jnp.float32) * 0.02
    kv_w = nrm(ks[15], (D, 2 * B_WIDTH + B_HEADS), jnp.float32) * D ** -0.5
    kv_b_f = jax.random.uniform(ks[16], (B_HEADS,), jnp.float32, minval=1.0, maxval=6.0)
    b_w_in = nrm(ks[17], (N_B, D, 2 * B_WIDTH), jnp.float32) * D ** -0.5
    b_w_out = nrm(ks[18], (N_B, B_WIDTH, D), jnp.float32) * (B_WIDTH ** -0.5 * DN_BETA)
    return {"x": x, "c": c, "w_mod": w_mod, "b_mod": b_mod, "ln_g": ln_g, "ln_b": ln_b,
            "a_w_in": a_w_in, "a_w_out": a_w_out, "a_lam_q1": a_lam_q1, "a_lam_k1": a_lam_k1,
            "a_lam_q2": a_lam_q2, "a_lam_k2": a_lam_k2, "a_subln_g": a_subln_g,
            "kv_w_mod": kv_w_mod, "kv_b_mod": kv_b_mod, "kv_w": kv_w, "kv_b_f": kv_b_f,
            "b_w_in": b_w_in, "b_w_out": b_w_out}


def reference(x, c, w_mod, b_mod, ln_g, ln_b, a_w_in, a_w_out, a_lam_q1, a_lam_k1,
              a_lam_q2, a_lam_k2, a_subln_g, kv_w_mod, kv_b_mod, kv_w, kv_b_f,
              b_w_in, b_w_out):
    c_act = jax.nn.silu(c)
    kv = None
    for l in range(DEPTH):
        mod = c_act @ w_mod[l] + b_mod[l]
        shift, scale, gate = jnp.split(mod, 3, axis=-1)
        h = x * (1.0 + scale[:, None]) + shift[:, None]
        if l < N_A:
            y = diff_attention(h, a_w_in[l], a_w_out[l], a_lam_q1[l], a_lam_k1[l],
                               a_lam_q2[l], a_lam_k2[l], a_subln_g[l], l)
        else:
            if kv is None:
                kv = shared_kv(x, c_act, kv_w_mod, kv_b_mod, kv_w, kv_b_f)
            k_sh, v_sh, F_sh = kv
            y = forgetting_attention(h, b_w_in[l - N_A], b_w_out[l - N_A], k_sh, v_sh, F_sh)
        x = layer_norm(DN_ALPHA * x + gate[:, None] * y, ln_g[l], ln_b[l])
    return x
```

```python
import functools
import math

import jax
import jax.numpy as jnp
from jax import lax
from jax.experimental import pallas as pl
from jax.experimental.pallas import tpu as pltpu

DEPTH = 4
N_A = DEPTH // 2
A_QK_DIM = 128
A_V_DIM = 2 * A_QK_DIM
B_HEAD_DIM = 128
NEG = -1e30
EPS = 1e-5
DN_ALPHA = (2.0 * DEPTH) ** 0.25

F32 = jnp.float32
BF16 = jnp.bfloat16

VMEM_LIMIT_BYTES = 56 * 1024 * 1024

MOD_TN = 1024
PROJ_TM = 1024
PROJ_TN = 1024
OUT_TM = 512
GATE_TM = 512
ATT_T = 512


def _params(*sem):
    return pltpu.CompilerParams(dimension_semantics=sem, vmem_limit_bytes=VMEM_LIMIT_BYTES)


def _silu(x):
    return x / (1.0 + jnp.exp(-x))


def _mod_kernel(c_ref, w_ref, b_ref, o_ref):
    c_act = _silu(c_ref[...]).astype(BF16)
    o_ref[...] = jnp.dot(c_act, w_ref[...].astype(BF16), preferred_element_type=F32) + b_ref[...]


def _modulation(c, w, b):
    L, D, N = w.shape
    B = c.shape[0]
    return pl.pallas_call(
        _mod_kernel,
        grid=(L, N // MOD_TN),
        in_specs=[
            pl.BlockSpec((B, D), lambda l, j: (0, 0)),
            pl.BlockSpec((None, D, MOD_TN), lambda l, j: (l, 0, j)),
            pl.BlockSpec((None, 1, MOD_TN), lambda l, j: (l, 0, j)),
        ],
        out_specs=pl.BlockSpec((None, B, MOD_TN), lambda l, j: (l, 0, j)),
        out_shape=jax.ShapeDtypeStruct((L, B, N), F32),
        compiler_params=_params("parallel", "parallel"),
        name="modulation",
    )(c, w, b.reshape(L, 1, N))


def _modproj_kernel(x_ref, shift_ref, scale_ref, w_ref, o_ref, h_ref):
    @pl.when(pl.program_id(2) == 0)
    def _():
        h_ref[...] = (x_ref[...] * (1.0 + scale_ref[...]) + shift_ref[...]).astype(BF16)

    o_ref[...] = jnp.dot(h_ref[...], w_ref[...], preferred_element_type=F32).astype(o_ref.dtype)


def _modproj(x, mod, shift_blk, scale_blk, w):
    B, S, D = x.shape
    N = w.shape[1]
    return pl.pallas_call(
        _modproj_kernel,
        grid=(B, S // PROJ_TM, N // PROJ_TN),
        in_specs=[
            pl.BlockSpec((None, PROJ_TM, D), lambda b, i, j: (b, i, 0)),
            pl.BlockSpec((None, 1, D), lambda b, i, j: (b, 0, shift_blk)),
            pl.BlockSpec((None, 1, D), lambda b, i, j: (b, 0, scale_blk)),
            pl.BlockSpec((D, PROJ_TN), lambda b, i, j: (0, j)),
        ],
        out_specs=pl.BlockSpec((None, PROJ_TM, PROJ_TN), lambda b, i, j: (b, i, j)),
        out_shape=jax.ShapeDtypeStruct((B, S, N), BF16),
        scratch_shapes=[pltpu.VMEM((PROJ_TM, D), BF16)],
        compiler_params=_params("parallel", "parallel", "arbitrary"),
        name="modproj",
    )(x, mod, mod, w)


def _outproj_ln_kernel(o_ref, w_ref, x_ref, gate_ref, g_ref, b_ref, y_ref):
    y = jnp.dot(o_ref[...], w_ref[...], preferred_element_type=F32)
    z = DN_ALPHA * x_ref[...] + gate_ref[...] * y
    mu = jnp.mean(z, axis=-1, keepdims=True)
    zc = z - mu
    var = jnp.mean(zc * zc, axis=-1, keepdims=True)
    y_ref[...] = zc * lax.rsqrt(var + EPS) * g_ref[...] + b_ref[...]


def _outproj_ln(o, w, x, mod, gate_blk, ln_g, ln_b):
    B, S, D = x.shape
    W = o.shape[2]
    return pl.pallas_call(
        _outproj_ln_kernel,
        grid=(B, S // OUT_TM),
        in_specs=[
            pl.BlockSpec((None, OUT_TM, W), lambda b, i: (b, i, 0)),
            pl.BlockSpec((W, D), lambda b, i: (0, 0)),
            pl.BlockSpec((None, OUT_TM, D), lambda b, i: (b, i, 0)),
            pl.BlockSpec((None, 1, D), lambda b, i: (b, 0, gate_blk)),
            pl.BlockSpec((1, D), lambda b, i: (0, 0)),
            pl.BlockSpec((1, D), lambda b, i: (0, 0)),
        ],
        out_specs=pl.BlockSpec((None, OUT_TM, D), lambda b, i: (b, i, 0)),
        out_shape=jax.ShapeDtypeStruct((B, S, D), F32),
        compiler_params=_params("parallel", "parallel"),
        name="outproj_ln",
    )(o, w, x, mod, ln_g.reshape(1, D), ln_b.reshape(1, D))


def _softmax_step(s, v, m, l, acc):
    m_new = jnp.maximum(m, jnp.max(s, axis=-1, keepdims=True))
    alpha = jnp.exp(m - m_new)
    p = jnp.exp(s - m_new)
    l = alpha * l + jnp.sum(p, axis=-1, keepdims=True)
    acc = alpha * acc + jnp.dot(p.astype(BF16), v, preferred_element_type=F32)
    return m_new, l, acc


def _qk(q, k):
    return lax.dot_general(q, k, (((1,), (1,)), ((), ())), preferred_element_type=F32)


def _causal_mask(t):
    row = lax.broadcasted_iota(jnp.int32, (t, t), 0)
    col = lax.broadcasted_iota(jnp.int32, (t, t), 1)
    return row >= col


def _diff_attn_kernel(slopes_ref, lq1_ref, lk1_ref, lq2_ref, lk2_ref, sg_ref,
                      q_ref, k_ref, v_ref, g_ref, o_ref, *, lam_init):
    t = ATT_T
    seq = q_ref.shape[0]
    slope = slopes_ref[pl.program_id(1)]
    lam = (jnp.exp(jnp.sum(lq1_ref[...] * lk1_ref[...], keepdims=True))
           - jnp.exp(jnp.sum(lq2_ref[...] * lk2_ref[...], keepdims=True)) + lam_init)
    col = lax.broadcasted_iota(jnp.int32, (1, t), 1).astype(F32)
    causal = _causal_mask(t)
    qk_scale = A_QK_DIM ** -0.5

    def q_block(i, _):
        q0 = pl.multiple_of(i * t, t)
        q = (q_ref[pl.ds(q0, t), :].astype(F32) * qk_scale).astype(BF16)
        q1, q2 = q[:, :A_QK_DIM], q[:, A_QK_DIM:]

        def kv_step(j, carry, masked):
            m1, l1, a1, m2, l2, a2 = carry
            k0 = pl.multiple_of(j * t, t)
            k = k_ref[pl.ds(k0, t), :]
            v = v_ref[pl.ds(k0, t), :]
            bias = slope * (col + (k0 - q0).astype(F32))
            s1 = _qk(q1, k[:, :A_QK_DIM]) + bias
            s2 = _qk(q2, k[:, A_QK_DIM:]) + bias
            if masked:
                s1 = jnp.where(causal, s1, NEG)
                s2 = jnp.where(causal, s2, NEG)
            m1, l1, a1 = _softmax_step(s1, v, m1, l1, a1)
            m2, l2, a2 = _softmax_step(s2, v, m2, l2, a2)
            return m1, l1, a1, m2, l2, a2

        def init():
            return (jnp.full((t, 1), NEG, F32), jnp.zeros((t, 1), F32), jnp.zeros((t, A_V_DIM), F32))

        carry = lax.fori_loop(0, i, lambda j, c: kv_step(j, c, False), init() + init())
        m1, l1, a1, m2, l2, a2 = kv_step(i, carry, True)
        o = a1 / l1 - lam * (a2 / l2)
        o = o * lax.rsqrt(jnp.mean(o * o, axis=-1, keepdims=True) + EPS) * sg_ref[...] * (1.0 - lam_init)
        o = o * _silu(g_ref[pl.ds(q0, t), :].astype(F32))
        o_ref[pl.ds(q0, t), :] = o.astype(o_ref.dtype)
        return 0

    lax.fori_loop(0, seq // t, q_block, 0)


def _diff_attention(proj, slopes, lq1, lk1, lq2, lk2, subln_g, lam_init):
    B, S, N = proj.shape
    W = N // 4
    H = W // A_V_DIM
    smem = pl.BlockSpec(memory_space=pltpu.SMEM)
    vec = pl.BlockSpec((1, A_QK_DIM), lambda b, h: (0, 0))
    head = lambda off: pl.BlockSpec((None, S, A_V_DIM), lambda b, h: (b, 0, off + h))
    return pl.pallas_call(
        functools.partial(_diff_attn_kernel, lam_init=lam_init),
        grid=(B, H),
        in_specs=[smem, vec, vec, vec, vec, pl.BlockSpec((1, A_V_DIM), lambda b, h: (0, 0)),
                  head(0), head(H), head(2 * H), head(3 * H)],
        out_specs=pl.BlockSpec((None, S, A_V_DIM), lambda b, h: (b, 0, h)),
        out_shape=jax.ShapeDtypeStruct((B, S, W), BF16),
        compiler_params=_params("parallel", "parallel"),
        name="diff_attn",
    )(slopes, lq1.reshape(1, -1), lk1.reshape(1, -1), lq2.reshape(1, -1), lk2.reshape(1, -1),
      subln_g.reshape(1, -1), proj, proj, proj, proj)


def _fgate_kernel(x_ref, shift_ref, scale_ref, wf_ref, bf_ref, o_ref, carry_ref):
    @pl.when(pl.program_id(1) == 0)
    def _():
        carry_ref[...] = jnp.zeros_like(carry_ref)

    tm = x_ref.shape[0]
    h = (x_ref[...] * (1.0 + scale_ref[...]) + shift_ref[...]).astype(BF16)
    u = _qk(wf_ref[...], h) + bf_ref[...]
    log_f = jnp.minimum(u, 0.0) - jnp.log1p(jnp.exp(-jnp.abs(u)))
    row = lax.broadcasted_iota(jnp.int32, (tm, tm), 0)
    col = lax.broadcasted_iota(jnp.int32, (tm, tm), 1)
    upper = (row <= col).astype(F32)
    o_ref[...] = carry_ref[...] + jnp.dot(log_f, upper, preferred_element_type=F32,
                                          precision=lax.Precision.HIGHEST)
    carry_ref[...] = carry_ref[...] + jnp.sum(log_f, axis=-1, keepdims=True)


def _forget_gate_cumsum(x, mod, wf_t, b_f):
    B, S, D = x.shape
    H = wf_t.shape[0]
    return pl.pallas_call(
        _fgate_kernel,
        grid=(B, S // GATE_TM),
        in_specs=[
            pl.BlockSpec((None, GATE_TM, D), lambda b, i: (b, i, 0)),
            pl.BlockSpec((None, 1, D), lambda b, i: (b, 0, 0)),
            pl.BlockSpec((None, 1, D), lambda b, i: (b, 0, 1)),
            pl.BlockSpec((H, D), lambda b, i: (0, 0)),
            pl.BlockSpec((H, 1), lambda b, i: (0, 0)),
        ],
        out_specs=pl.BlockSpec((None, H, GATE_TM), lambda b, i: (b, 0, i)),
        out_shape=jax.ShapeDtypeStruct((B, H, S), F32),
        scratch_shapes=[pltpu.VMEM((H, 1), F32)],
        compiler_params=_params("parallel", "arbitrary"),
        name="fgate_scan",
    )(x, mod, mod, wf_t, b_f.reshape(H, 1))


def _fox_attn_kernel(q_ref, k_ref, v_ref, g_ref, f_ref, o_ref):
    t = ATT_T
    seq = q_ref.shape[0]
    causal = _causal_mask(t)
    qk_scale = B_HEAD_DIM ** -0.5

    def q_block(i, _):
        q0 = pl.multiple_of(i * t, t)
        q = (q_ref[pl.ds(q0, t), :].astype(F32) * qk_scale).astype(BF16)

        def kv_step(j, carry, masked):
            k0 = pl.multiple_of(j * t, t)
            s = _qk(q, k_ref[pl.ds(k0, t), :]) - f_ref[:, pl.ds(k0, t)]
            if masked:
                s = jnp.where(causal, s, NEG)
            return _softmax_step(s, v_ref[pl.ds(k0, t), :], *carry)

        init = (jnp.full((t, 1), NEG, F32), jnp.zeros((t, 1), F32), jnp.zeros((t, B_HEAD_DIM), F32))
        carry = lax.fori_loop(0, i, lambda j, c: kv_step(j, c, False), init)
        _, l, acc = kv_step(i, carry, True)
        o = (acc / l) * _silu(g_ref[pl.ds(q0, t), :].astype(F32))
        o_ref[pl.ds(q0, t), :] = o.astype(o_ref.dtype)
        return 0

    lax.fori_loop(0, seq // t, q_block, 0)


def _fox_attention(qg, kv, f_cum):
    B, S, N = qg.shape
    W = N // 2
    H = W // B_HEAD_DIM
    head = lambda off: pl.BlockSpec((None, S, B_HEAD_DIM), lambda b, h: (b, 0, off + h))
    return pl.pallas_call(
        _fox_attn_kernel,
        grid=(B, H),
        in_specs=[head(0), head(0), head(H), head(H),
                  pl.BlockSpec((None, None, 1, S), lambda b, h: (b, h, 0, 0))],
        out_specs=pl.BlockSpec((None, S, B_HEAD_DIM), lambda b, h: (b, 0, h)),
        out_shape=jax.ShapeDtypeStruct((B, S, W), BF16),
        compiler_params=_params("parallel", "parallel"),
        name="fox_attn",
    )(qg, kv, kv, qg, f_cum.reshape(B, H, 1, S))


def kernel(x, c, w_mod, b_mod, ln_g, ln_b, a_w_in, a_w_out, a_lam_q1, a_lam_k1, a_lam_q2, a_lam_k2,
           a_subln_g, kv_w_mod, kv_b_mod, kv_w, kv_b_f, b_w_in, b_w_out):
    B, S, D = x.shape
    a_heads = a_w_out.shape[1] // A_V_DIM
    b_width = b_w_out.shape[1]
    slopes = jnp.asarray([2.0 ** (-8.0 * (h + 1) / a_heads) for h in range(a_heads)], dtype=F32)

    mod = _modulation(c, w_mod, b_mod)
    mod_kv = _modulation(c, kv_w_mod[None], kv_b_mod[None])[0]
    mod_kv = mod_kv.reshape(B, 1, 2 * D)

    kv = f_cum = None
    for l in range(DEPTH):
        mod_l = mod[l].reshape(B, 1, 3 * D)
        if l < N_A:
            lam_init = 0.8 - 0.6 * math.exp(-0.3 * l)
            proj = _modproj(x, mod_l, 0, 1, a_w_in[l].astype(BF16))
            o = _diff_attention(proj, slopes, a_lam_q1[l], a_lam_k1[l], a_lam_q2[l], a_lam_k2[l],
                                a_subln_g[l], lam_init)
            w_out = a_w_out[l]
        else:
            if kv is None:
                kv = _modproj(x, mod_kv, 0, 1, kv_w[:, :2 * b_width].astype(BF16))
                f_cum = _forget_gate_cumsum(x, mod_kv, kv_w[:, 2 * b_width:].T.astype(BF16), kv_b_f)
            qg = _modproj(x, mod_l, 0, 1, b_w_in[l - N_A].astype(BF16))
            o = _fox_attention(qg, kv, f_cum)
            w_out = b_w_out[l - N_A]
        x = _outproj_ln(o, w_out.astype(BF16), x, mod_l, 2, ln_g[l], ln_b[l])
    return x
```

```python
import functools
import math

import jax
import jax.numpy as jnp
from jax import lax
from jax.experimental import pallas as pl
from jax.experimental.pallas import tpu as pltpu

DEPTH = 4
N_A = DEPTH // 2
A_QK_DIM = 128
A_V_DIM = 2 * A_QK_DIM
B_HEAD_DIM = 128
NEG = -1e30
EPS = 1e-5
DN_ALPHA = (2.0 * DEPTH) ** 0.25
LOG2E = math.log2(math.e)

F32 = jnp.float32
BF16 = jnp.bfloat16

VMEM_LIMIT_BYTES = 56 * 1024 * 1024

MOD_TN = 1024
PROJ_TM = 1024
PROJ_TN = 1024
OUT_TM = 512
GATE_TM = 512
ATT_T = 512


def _params(*sem):
    return pltpu.CompilerParams(dimension_semantics=sem, vmem_limit_bytes=VMEM_LIMIT_BYTES)


def _silu(x):
    return x / (1.0 + jnp.exp(-x))


def _mod_kernel(c_ref, w_ref, b_ref, o_ref):
    c_act = _silu(c_ref[...]).astype(BF16)
    o_ref[...] = jnp.dot(c_act, w_ref[...].astype(BF16), preferred_element_type=F32) + b_ref[...]


def _modulation(c, w, b):
    L, D, N = w.shape
    B = c.shape[0]
    return pl.pallas_call(
        _mod_kernel,
        grid=(L, N // MOD_TN),
        in_specs=[
            pl.BlockSpec((B, D), lambda l, j: (0, 0)),
            pl.BlockSpec((None, D, MOD_TN), lambda l, j: (l, 0, j)),
            pl.BlockSpec((None, 1, MOD_TN), lambda l, j: (l, 0, j)),
        ],
        out_specs=pl.BlockSpec((None, B, MOD_TN), lambda l, j: (l, 0, j)),
        out_shape=jax.ShapeDtypeStruct((L, B, N), F32),
        compiler_params=_params("parallel", "parallel"),
        name="modulation",
    )(c, w, b.reshape(L, 1, N))


def _modproj_kernel(x_ref, shift_ref, scale_ref, w_ref, o_ref, h_ref):
    @pl.when(pl.program_id(2) == 0)
    def _():
        h_ref[...] = (x_ref[...] * (1.0 + scale_ref[...]) + shift_ref[...]).astype(BF16)

    o_ref[...] = jnp.dot(h_ref[...], w_ref[...], preferred_element_type=F32).astype(o_ref.dtype)


def _modproj(x, mod, shift_blk, scale_blk, w):
    B, S, D = x.shape
    N = w.shape[1]
    return pl.pallas_call(
        _modproj_kernel,
        grid=(B, S // PROJ_TM, N // PROJ_TN),
        in_specs=[
            pl.BlockSpec((None, PROJ_TM, D), lambda b, i, j: (b, i, 0)),
            pl.BlockSpec((None, 1, D), lambda b, i, j: (b, 0, shift_blk)),
            pl.BlockSpec((None, 1, D), lambda b, i, j: (b, 0, scale_blk)),
            pl.BlockSpec((D, PROJ_TN), lambda b, i, j: (0, j)),
        ],
        out_specs=pl.BlockSpec((None, PROJ_TM, PROJ_TN), lambda b, i, j: (b, i, j)),
        out_shape=jax.ShapeDtypeStruct((B, S, N), BF16),
        scratch_shapes=[pltpu.VMEM((PROJ_TM, D), BF16)],
        compiler_params=_params("parallel", "parallel", "arbitrary"),
        name="modproj",
    )(x, mod, mod, w)


def _outproj_ln_kernel(o_ref, w_ref, x_ref, gate_ref, g_ref, b_ref, y_ref):
    y = jnp.dot(o_ref[...], w_ref[...], preferred_element_type=F32)
    z = DN_ALPHA * x_ref[...] + gate_ref[...] * y
    mu = jnp.mean(z, axis=-1, keepdims=True)
    zc = z - mu
    var = jnp.mean(zc * zc, axis=-1, keepdims=True)
    y_ref[...] = zc * lax.rsqrt(var + EPS) * g_ref[...] + b_ref[...]


def _outproj_ln(o, w, x, mod, gate_blk, ln_g, ln_b):
    B, S, D = x.shape
    W = o.shape[2]
    return pl.pallas_call(
        _outproj_ln_kernel,
        grid=(B, S // OUT_TM),
        in_specs=[
            pl.BlockSpec((None, OUT_TM, W), lambda b, i: (b, i, 0)),
            pl.BlockSpec((W, D), lambda b, i: (0, 0)),
            pl.BlockSpec((None, OUT_TM, D), lambda b, i: (b, i, 0)),
            pl.BlockSpec((None, 1, D), lambda b, i: (b, 0, gate_blk)),
            pl.BlockSpec((1, D), lambda b, i: (0, 0)),
            pl.BlockSpec((1, D), lambda b, i: (0, 0)),
        ],
        out_specs=pl.BlockSpec((None, OUT_TM, D), lambda b, i: (b, i, 0)),
        out_shape=jax.ShapeDtypeStruct((B, S, D), F32),
        compiler_params=_params("parallel", "parallel"),
        name="outproj_ln",
    )(o, w, x, mod, ln_g.reshape(1, D), ln_b.reshape(1, D))


def _qk(q, k):
    return lax.dot_general(q, k, (((1,), (1,)), ((), ())), preferred_element_type=F32)


def _causal_mask(t):
    row = lax.broadcasted_iota(jnp.int32, (t, t), 0)
    col = lax.broadcasted_iota(jnp.int32, (t, t), 1)
    return row >= col


def _causal_softmax_pv(q, k_ref, v_ref, bias, lo, causal):
    t = q.shape[0]
    s_d = jnp.where(causal, _qk(q, k_ref[lo:lo + t, :]) + bias[:, lo:lo + t], NEG)
    m = jnp.max(s_d, axis=-1, keepdims=True)
    if lo:
        s_f = _qk(q, k_ref[0:lo, :]) + bias[:, 0:lo]
        m = jnp.maximum(m, jnp.max(s_f, axis=-1, keepdims=True))
    p_d = jnp.exp2(s_d - m)
    l = jnp.sum(p_d, axis=-1, keepdims=True)
    acc = jnp.dot(p_d.astype(BF16), v_ref[lo:lo + t, :], preferred_element_type=F32)
    if lo:
        p_f = jnp.exp2(s_f - m)
        l = l + jnp.sum(p_f, axis=-1, keepdims=True)
        acc = acc + jnp.dot(p_f.astype(BF16), v_ref[0:lo, :], preferred_element_type=F32)
    return acc / l


def _diff_attn_kernel(slopes_ref, lq1_ref, lk1_ref, lq2_ref, lk2_ref, sg_ref,
                      q_ref, k_ref, v_ref, g_ref, o_ref, *, lam_init):
    t = ATT_T
    seq = q_ref.shape[0]
    slope2 = slopes_ref[pl.program_id(1)] * LOG2E
    lam = (jnp.exp(jnp.sum(lq1_ref[...] * lk1_ref[...], keepdims=True))
           - jnp.exp(jnp.sum(lq2_ref[...] * lk2_ref[...], keepdims=True)) + lam_init)
    pos = lax.broadcasted_iota(jnp.int32, (1, seq), 1)
    causal = _causal_mask(t)
    k1_ref, k2_ref = k_ref.at[:, :A_QK_DIM], k_ref.at[:, A_QK_DIM:]

    for i in range(seq // t):
        lo = i * t
        q = (q_ref[lo:lo + t, :].astype(F32) * (A_QK_DIM ** -0.5 * LOG2E)).astype(BF16)
        bias = slope2 * (pos - lo).astype(F32)
        o1 = _causal_softmax_pv(q[:, :A_QK_DIM], k1_ref, v_ref, bias, lo, causal)
        o2 = _causal_softmax_pv(q[:, A_QK_DIM:], k2_ref, v_ref, bias, lo, causal)
        o = o1 - lam * o2
        o = o * lax.rsqrt(jnp.mean(o * o, axis=-1, keepdims=True) + EPS) * sg_ref[...] * (1.0 - lam_init)
        o = o * _silu(g_ref[lo:lo + t, :].astype(F32))
        o_ref[lo:lo + t, :] = o.astype(o_ref.dtype)


def _diff_attention(proj, slopes, lq1, lk1, lq2, lk2, subln_g, lam_init):
    B, S, N = proj.shape
    W = N // 4
    H = W // A_V_DIM
    smem = pl.BlockSpec(memory_space=pltpu.SMEM)
    vec = pl.BlockSpec((1, A_QK_DIM), lambda b, h: (0, 0))
    head = lambda off: pl.BlockSpec((None, S, A_V_DIM), lambda b, h: (b, 0, off + h))
    return pl.pallas_call(
        functools.partial(_diff_attn_kernel, lam_init=lam_init),
        grid=(B, H),
        in_specs=[smem, vec, vec, vec, vec, pl.BlockSpec((1, A_V_DIM), lambda b, h: (0, 0)),
                  head(0), head(H), head(2 * H), head(3 * H)],
        out_specs=pl.BlockSpec((None, S, A_V_DIM), lambda b, h: (b, 0, h)),
        out_shape=jax.ShapeDtypeStruct((B, S, W), BF16),
        compiler_params=_params("parallel", "parallel"),
        name="diff_attn",
    )(slopes, lq1.reshape(1, -1), lk1.reshape(1, -1), lq2.reshape(1, -1), lk2.reshape(1, -1),
      subln_g.reshape(1, -1), proj, proj, proj, proj)


def _fgate_kernel(x_ref, shift_ref, scale_ref, wf_ref, bf_ref, o_ref, carry_ref):
    @pl.when(pl.program_id(1) == 0)
    def _():
        carry_ref[...] = jnp.zeros_like(carry_ref)

    tm = x_ref.shape[0]
    h = (x_ref[...] * (1.0 + scale_ref[...]) + shift_ref[...]).astype(BF16)
    u = _qk(wf_ref[...], h) + bf_ref[...]
    log_f = jnp.minimum(u, 0.0) - jnp.log1p(jnp.exp(-jnp.abs(u)))
    row = lax.broadcasted_iota(jnp.int32, (tm, tm), 0)
    col = lax.broadcasted_iota(jnp.int32, (tm, tm), 1)
    upper = (row <= col).astype(F32)
    o_ref[...] = carry_ref[...] + jnp.dot(log_f, upper, preferred_element_type=F32,
                                          precision=lax.Precision.HIGHEST)
    carry_ref[...] = carry_ref[...] + jnp.sum(log_f, axis=-1, keepdims=True)


def _forget_gate_cumsum(x, mod, wf_t, b_f):
    B, S, D = x.shape
    H = wf_t.shape[0]
    return pl.pallas_call(
        _fgate_kernel,
        grid=(B, S // GATE_TM),
        in_specs=[
            pl.BlockSpec((None, GATE_TM, D), lambda b, i: (b, i, 0)),
            pl.BlockSpec((None, 1, D), lambda b, i: (b, 0, 0)),
            pl.BlockSpec((None, 1, D), lambda b, i: (b, 0, 1)),
            pl.BlockSpec((H, D), lambda b, i: (0, 0)),
            pl.BlockSpec((H, 1), lambda b, i: (0, 0)),
        ],
        out_specs=pl.BlockSpec((None, H, GATE_TM), lambda b, i: (b, 0, i)),
        out_shape=jax.ShapeDtypeStruct((B, H, S), F32),
        scratch_shapes=[pltpu.VMEM((H, 1), F32)],
        compiler_params=_params("parallel", "arbitrary"),
        name="fgate_scan",
    )(x, mod, mod, wf_t, b_f.reshape(H, 1))


def _fox_attn_kernel(q_ref, k_ref, v_ref, g_ref, f_ref, o_ref):
    t = ATT_T
    seq = q_ref.shape[0]
    causal = _causal_mask(t)
    bias = f_ref[...] * (-LOG2E)
    for i in range(seq // t):
        lo = i * t
        q = (q_ref[lo:lo + t, :].astype(F32) * (B_HEAD_DIM ** -0.5 * LOG2E)).astype(BF16)
        o = _causal_softmax_pv(q, k_ref, v_ref, bias, lo, causal)
        o = o * _silu(g_ref[lo:lo + t, :].astype(F32))
        o_ref[lo:lo + t, :] = o.astype(o_ref.dtype)


def _fox_attention(qg, kv, f_cum):
    B, S, N = qg.shape
    W = N // 2
    H = W // B_HEAD_DIM
    head = lambda off: pl.BlockSpec((None, S, B_HEAD_DIM), lambda b, h: (b, 0, off + h))
    return pl.pallas_call(
        _fox_attn_kernel,
        grid=(B, H),
        in_specs=[head(0), head(0), head(H), head(H),
                  pl.BlockSpec((None, None, 1, S), lambda b, h: (b, h, 0, 0))],
        out_specs=pl.BlockSpec((None, S, B_HEAD_DIM), lambda b, h: (b, 0, h)),
        out_shape=jax.ShapeDtypeStruct((B, S, W), BF16),
        compiler_params=_params("parallel", "parallel"),
        name="fox_attn",
    )(qg, kv, kv, qg, f_cum.reshape(B, H, 1, S))


def kernel(x, c, w_mod, b_mod, ln_g, ln_b, a_w_in, a_w_out, a_lam_q1, a_lam_k1, a_lam_q2, a_lam_k2,
           a_subln_g, kv_w_mod, kv_b_mod, kv_w, kv_b_f, b_w_in, b_w_out):
    B, S, D = x.shape
    a_heads = a_w_out.shape[1] // A_V_DIM
    b_width = b_w_out.shape[1]
    slopes = jnp.asarray([2.0 ** (-8.0 * (h + 1) / a_heads) for h in range(a_heads)], dtype=F32)

    mod = _modulation(c, w_mod, b_mod)
    mod_kv = _modulation(c, kv_w_mod[None], kv_b_mod[None])[0]
    mod_kv = mod_kv.reshape(B, 1, 2 * D)

    kv = f_cum = None
    for l in range(DEPTH):
        mod_l = mod[l].reshape(B, 1, 3 * D)
        if l < N_A:
            lam_init = 0.8 - 0.6 * math.exp(-0.3 * l)
            proj = _modproj(x, mod_l, 0, 1, a_w_in[l].astype(BF16))
            o = _diff_attention(proj, slopes, a_lam_q1[l], a_lam_k1[l], a_lam_q2[l], a_lam_k2[l],
                                a_subln_g[l], lam_init)
            w_out = a_w_out[l]
        else:
            if kv is None:
                kv = _modproj(x, mod_kv, 0, 1, kv_w[:, :2 * b_width].astype(BF16))
                f_cum = _forget_gate_cumsum(x, mod_kv, kv_w[:, 2 * b_width:].T.astype(BF16), kv_b_f)
            qg = _modproj(x, mod_l, 0, 1, b_w_in[l - N_A].astype(BF16))
            o = _fox_attention(qg, kv, f_cum)
            w_out = b_w_out[l - N_A]
        x = _outproj_ln(o, w_out.astype(BF16), x, mod_l, 2, ln_g[l], ln_b[l])
    return x
```

```python
import functools
import math

import jax
import jax.numpy as jnp
from jax import lax
from jax.experimental import pallas as pl
from jax.experimental.pallas import tpu as pltpu

DEPTH = 4
N_A = DEPTH // 2
A_QK_DIM = 128
A_V_DIM = 2 * A_QK_DIM
B_HEAD_DIM = 128
NEG = -1e30
EPS = 1e-5
DN_ALPHA = (2.0 * DEPTH) ** 0.25
LOG2E = math.log2(math.e)

F32 = jnp.float32
BF16 = jnp.bfloat16

VMEM_LIMIT_BYTES = 56 * 1024 * 1024

MOD_TN = 1024
PROJ_TM = 1024
PROJ_TN = 1024
OUT_TM = 512
OUT_SPLIT = 2
GATE_TM = 512
ATT_T = 256


def _params(*sem):
    return pltpu.CompilerParams(dimension_semantics=sem, vmem_limit_bytes=VMEM_LIMIT_BYTES)


def _silu(x):
    return x / (1.0 + jnp.exp(-x))


def _mod_kernel(c_ref, w_ref, b_ref, o_ref):
    c_act = _silu(c_ref[...]).astype(BF16)
    o_ref[...] = jnp.dot(c_act, w_ref[...].astype(BF16), preferred_element_type=F32) + b_ref[...]


def _modulation(c, w, b):
    L, D, N = w.shape
    B = c.shape[0]
    return pl.pallas_call(
        _mod_kernel,
        grid=(L, N // MOD_TN),
        in_specs=[
            pl.BlockSpec((B, D), lambda l, j: (0, 0)),
            pl.BlockSpec((None, D, MOD_TN), lambda l, j: (l, 0, j)),
            pl.BlockSpec((None, 1, MOD_TN), lambda l, j: (l, 0, j)),
        ],
        out_specs=pl.BlockSpec((None, B, MOD_TN), lambda l, j: (l, 0, j)),
        out_shape=jax.ShapeDtypeStruct((L, B, N), F32),
        compiler_params=_params("parallel", "parallel"),
        name="modulation",
    )(c, w, b.reshape(L, 1, N))


def _modproj_kernel(x_ref, shift_ref, scale_ref, w_ref, o_ref):
    h = (x_ref[...] * (1.0 + scale_ref[...]) + shift_ref[...]).astype(BF16)
    o_ref[...] = jnp.dot(h, w_ref[...], preferred_element_type=F32).astype(o_ref.dtype)


def _modproj(x, mod, shift_blk, scale_blk, w):
    B, S, D = x.shape
    N = w.shape[1]
    return pl.pallas_call(
        _modproj_kernel,
        grid=(B, S // PROJ_TM, N // PROJ_TN),
        in_specs=[
            pl.BlockSpec((None, PROJ_TM, D), lambda b, i, j: (b, i, 0)),
            pl.BlockSpec((None, 1, D), lambda b, i, j: (b, 0, shift_blk)),
            pl.BlockSpec((None, 1, D), lambda b, i, j: (b, 0, scale_blk)),
            pl.BlockSpec((D, PROJ_TN), lambda b, i, j: (0, j)),
        ],
        out_specs=pl.BlockSpec((None, PROJ_TM, PROJ_TN), lambda b, i, j: (b, i, j)),
        out_shape=jax.ShapeDtypeStruct((B, S, N), BF16),
        compiler_params=_params("parallel", "parallel", "parallel"),
        name="modproj",
    )(x, mod, mod, w)


def _outproj_ln_kernel(o_ref, w_ref, x_ref, gate_ref, g_ref, b_ref, y_ref):
    rows = o_ref.shape[0] // OUT_SPLIT
    for r in range(OUT_SPLIT):
        sl = slice(r * rows, (r + 1) * rows)
        y = jnp.dot(o_ref[sl, :], w_ref[...], preferred_element_type=F32)
        z = DN_ALPHA * x_ref[sl, :] + gate_ref[...] * y
        mu = jnp.mean(z, axis=-1, keepdims=True)
        zc = z - mu
        var = jnp.mean(zc * zc, axis=-1, keepdims=True)
        y_ref[sl, :] = zc * lax.rsqrt(var + EPS) * g_ref[...] + b_ref[...]


def _outproj_ln(o, w, x, mod, gate_blk, ln_g, ln_b):
    B, S, D = x.shape
    W = o.shape[2]
    return pl.pallas_call(
        _outproj_ln_kernel,
        grid=(B, S // OUT_TM),
        in_specs=[
            pl.BlockSpec((None, OUT_TM, W), lambda b, i: (b, i, 0)),
            pl.BlockSpec((W, D), lambda b, i: (0, 0)),
            pl.BlockSpec((None, OUT_TM, D), lambda b, i: (b, i, 0)),
            pl.BlockSpec((None, 1, D), lambda b, i: (b, 0, gate_blk)),
            pl.BlockSpec((1, D), lambda b, i: (0, 0)),
            pl.BlockSpec((1, D), lambda b, i: (0, 0)),
        ],
        out_specs=pl.BlockSpec((None, OUT_TM, D), lambda b, i: (b, i, 0)),
        out_shape=jax.ShapeDtypeStruct((B, S, D), F32),
        compiler_params=_params("parallel", "parallel"),
        name="outproj_ln",
    )(o, w, x, mod, ln_g.reshape(1, D), ln_b.reshape(1, D))


def _qk(q, k):
    return lax.dot_general(q, k, (((1,), (1,)), ((), ())), preferred_element_type=F32)


def _causal_mask(t):
    row = lax.broadcasted_iota(jnp.int32, (t, t), 0)
    col = lax.broadcasted_iota(jnp.int32, (t, t), 1)
    return row >= col


def _split3(x):
    hi = x.astype(BF16)
    r = x - hi.astype(F32)
    mid = r.astype(BF16)
    return hi, mid, (r - mid.astype(F32)).astype(BF16)


def _softmax_rows(q, k_ref, lo, causal, p_ref, l_ref):
    t = q.shape[0]
    s_d = jnp.where(causal, _qk(q, k_ref[lo:lo + t, :]), NEG)
    m = jnp.max(s_d, axis=-1, keepdims=True)
    if lo:
        s_f = _qk(q, k_ref[0:lo, :])
        m = jnp.maximum(m, jnp.max(s_f, axis=-1, keepdims=True))
    p_d = jnp.exp2(s_d - m)
    p_ref[:, lo:lo + t] = p_d.astype(BF16)
    if lo:
        p_f = jnp.exp2(s_f - m)
        p_ref[:, 0:lo] = p_f.astype(BF16)
    if l_ref is not None:
        l = jnp.sum(p_d, axis=-1, keepdims=True)
        if lo:
            l = l + jnp.sum(p_f, axis=-1, keepdims=True)
        l_ref[...] = l


def _p_scratch(seq, maps, sums):
    t = ATT_T
    shapes = []
    for _ in range(2 * maps):
        for i in range(seq // t):
            shapes.append(pltpu.VMEM((t, (i + 1) * t), BF16))
            if sums:
                shapes.append(pltpu.VMEM((t, 1), F32))
    return shapes


def _split_scratch(refs, seq, maps, sums):
    rows = seq // ATT_T
    it = iter(refs)
    return [[[(next(it), next(it) if sums else None) for _ in range(rows)] for _ in range(maps)]
            for _ in range(2)]


def _two_stage(n, scratch, stage1, stage2, init):
    @pl.when(n == 0)
    def _():
        init()
        for per_map in scratch[1]:
            for p_ref, l_ref in per_map:
                p_ref[...] = jnp.ones_like(p_ref)
                if l_ref is not None:
                    l_ref[...] = jnp.ones_like(l_ref)

    for parity in (0, 1):
        @pl.when(n % 2 == parity)
        def _():
            stage1(scratch[parity])
            stage2(scratch[1 - parity])


def _diff_attn_kernel(slopes_ref, lq1_ref, lk1_ref, lq2_ref, lk2_ref, sg_ref,
                      q_ref, k_ref, v_ref, g_ref, o_ref, *scratch, lam_init, heads, steps):
    t = ATT_T
    d = A_QK_DIM
    seq = q_ref.shape[0]
    n = pl.program_id(0)
    kaug_refs, scratch = scratch[:2], _split_scratch(scratch[2:], seq, 2, True)
    lane = lax.broadcasted_iota(jnp.int32, (1, d), 1)

    def init():
        pos = lax.broadcasted_iota(jnp.int32, (seq, d), 0)
        col = lax.broadcasted_iota(jnp.int32, (seq, d), 1)
        a = (pos >> 8).astype(F32)
        b = (pos & 255).astype(F32)
        tile = jnp.where(col < 3, a, jnp.where(col < 6, b, 0.0)).astype(BF16)
        for kaug_ref in kaug_refs:
            kaug_ref[:, d:] = tile

    def stage1(sets):
        slope2 = slopes_ref[jnp.minimum(n, steps - 1) % heads] * LOG2E
        pieces = [x.astype(F32) for x in _split3(jnp.full((1, d), slope2, F32))]
        q_bias = jnp.zeros((1, d), F32)
        for j, x in enumerate(pieces):
            q_bias = jnp.where(lane == j, 256.0 * x, jnp.where(lane == 3 + j, x, q_bias))
        q_bias = jnp.broadcast_to(q_bias.astype(BF16), (t, d))
        causal = _causal_mask(t)
        for c in range(2):
            kaug_refs[c][:, :d] = k_ref[:, c * d:(c + 1) * d]
        for i in range(seq // t):
            lo = i * t
            q = (q_ref[lo:lo + t, :].astype(F32) * (d ** -0.5 * LOG2E)).astype(BF16)
            for c in range(2):
                q_aug = jnp.concatenate([q[:, c * d:(c + 1) * d], q_bias], axis=1)
                _softmax_rows(q_aug, kaug_refs[c], lo, causal, *sets[c][i])

    def stage2(sets):
        lam = (jnp.exp(jnp.sum(lq1_ref[...] * lk1_ref[...], keepdims=True))
               - jnp.exp(jnp.sum(lq2_ref[...] * lk2_ref[...], keepdims=True)) + lam_init)
        sub_gain = sg_ref[...] * (1.0 - lam_init)
        for i in range(seq // t):
            lo = i * t
            v = v_ref[0:lo + t, :]
            a1, a2 = (jnp.dot(sets[c][i][0][...], v, preferred_element_type=F32) for c in range(2))
            o = a1 * (1.0 / sets[0][i][1][...]) - a2 * (lam / sets[1][i][1][...])
            o = o * lax.rsqrt(jnp.mean(o * o, axis=-1, keepdims=True) + EPS) * sub_gain
            o = o * _silu(g_ref[lo:lo + t, :].astype(F32))
            o_ref[lo:lo + t, :] = o.astype(o_ref.dtype)

    _two_stage(n, scratch, stage1, stage2, init)


def _diff_attention(proj, slopes, lq1, lk1, lq2, lk2, subln_g, lam_init):
    B, S, N = proj.shape
    W = N // 4
    H = W // A_V_DIM
    steps = B * H
    smem = pl.BlockSpec(memory_space=pltpu.SMEM)
    vec = pl.BlockSpec((1, A_QK_DIM), lambda n: (0, 0))
    s1 = lambda off: pl.BlockSpec(
        (None, S, A_V_DIM), lambda n: (jnp.minimum(n, steps - 1) // H, 0, off + jnp.minimum(n, steps - 1) % H))
    s2 = lambda off: pl.BlockSpec(
        (None, S, A_V_DIM), lambda n: (jnp.maximum(n - 1, 0) // H, 0, off + jnp.maximum(n - 1, 0) % H))
    return pl.pallas_call(
        functools.partial(_diff_attn_kernel, lam_init=lam_init, heads=H, steps=steps),
        grid=(steps + 1,),
        in_specs=[smem, vec, vec, vec, vec, pl.BlockSpec((1, A_V_DIM), lambda n: (0, 0)),
                  s1(0), s1(H), s2(2 * H), s2(3 * H)],
        out_specs=s2(0),
        out_shape=jax.ShapeDtypeStruct((B, S, W), BF16),
        scratch_shapes=[pltpu.VMEM((S, 2 * A_QK_DIM), BF16)] * 2 + _p_scratch(S, 2, True),
        compiler_params=_params("arbitrary"),
        name="diff_attn",
    )(slopes, lq1.reshape(1, -1), lk1.reshape(1, -1), lq2.reshape(1, -1), lk2.reshape(1, -1),
      subln_g.reshape(1, -1), proj, proj, proj, proj)


def _fgate_kernel(x_ref, shift_ref, scale_ref, wf_ref, bf_ref, e_ref, carry_ref, *, heads):
    @pl.when(pl.program_id(1) == 0)
    def _():
        carry_ref[...] = jnp.zeros_like(carry_ref)

    tm, lanes = e_ref.shape
    h = (x_ref[...] * (1.0 + scale_ref[...]) + shift_ref[...]).astype(BF16)
    u = jnp.dot(h, wf_ref[...], preferred_element_type=F32) + bf_ref[...]
    log_f = jnp.minimum(u, 0.0) - jnp.log1p(jnp.exp(-jnp.abs(u)))
    row = lax.broadcasted_iota(jnp.int32, (tm, tm), 0)
    col = lax.broadcasted_iota(jnp.int32, (tm, tm), 1)
    lower = (row >= col).astype(F32)
    f_cum = carry_ref[...] + jnp.dot(lower, log_f, preferred_element_type=F32,
                                     precision=lax.Precision.HIGHEST)
    carry_ref[...] = f_cum[tm - 1:tm, :]
    hi, mid, lo = _split3(f_cum * (-LOG2E))
    lane = lax.broadcasted_iota(jnp.int32, (tm, lanes), 1)
    zero = jnp.zeros_like(hi)
    e_ref[...] = jnp.where(lane < heads, hi, jnp.where(lane < 2 * heads, mid,
                                                       jnp.where(lane < 3 * heads, lo, zero)))


def _forget_gate_bias(x, mod, wf3, bf3, heads):
    B, S, D = x.shape
    lanes = wf3.shape[1]
    return pl.pallas_call(
        functools.partial(_fgate_kernel, heads=heads),
        grid=(B, S // GATE_TM),
        in_specs=[
            pl.BlockSpec((None, GATE_TM, D), lambda b, i: (b, i, 0)),
            pl.BlockSpec((None, 1, D), lambda b, i: (b, 0, 0)),
            pl.BlockSpec((None, 1, D), lambda b, i: (b, 0, 1)),
            pl.BlockSpec((D, lanes), lambda b, i: (0, 0)),
            pl.BlockSpec((1, lanes), lambda b, i: (0, 0)),
        ],
        out_specs=pl.BlockSpec((None, GATE_TM, lanes), lambda b, i: (b, i, 0)),
        out_shape=jax.ShapeDtypeStruct((B, S, lanes), BF16),
        scratch_shapes=[pltpu.VMEM((1, lanes), F32)],
        compiler_params=_params("parallel", "arbitrary"),
        name="fgate_scan",
    )(x, mod, mod, wf3, bf3)


def _fox_attn_kernel(q_ref, k_ref, e_ref, v_ref, g_ref, o_ref, kaug_ref, vaug_ref, *scratch, heads, steps):
    t = ATT_T
    d = B_HEAD_DIM
    seq = q_ref.shape[0]
    n = pl.program_id(0)
    scratch = _split_scratch(scratch, seq, 1, False)

    def init():
        col = lax.broadcasted_iota(jnp.int32, (seq, d), 1)
        vaug_ref[:, d:] = jnp.where(col == 0, 1.0, 0.0).astype(BF16)

    def stage1(sets):
        h = jnp.minimum(n, steps - 1) % heads
        lane = lax.broadcasted_iota(jnp.int32, (1, d), 1)
        pick = jnp.where(lane == h, 1.0, jnp.where(lane == heads + h, 1.0,
                                                    jnp.where(lane == 2 * heads + h, 1.0, 0.0)))
        q_bias = jnp.broadcast_to(pick.astype(BF16), (t, d))
        causal = _causal_mask(t)
        kaug_ref[:, :d] = k_ref[...]
        kaug_ref[:, d:] = e_ref[...]
        for i in range(seq // t):
            lo = i * t
            q = (q_ref[lo:lo + t, :].astype(F32) * (d ** -0.5 * LOG2E)).astype(BF16)
            _softmax_rows(jnp.concatenate([q, q_bias], axis=1), kaug_ref, lo, causal, *sets[0][i])

    def stage2(sets):
        vaug_ref[:, :d] = v_ref[...]
        for i in range(seq // t):
            lo = i * t
            acc = jnp.dot(sets[0][i][0][...], vaug_ref[0:lo + t, :], preferred_element_type=F32)
            o = acc[:, :d] / acc[:, d:d + 1] * _silu(g_ref[lo:lo + t, :].astype(F32))
            o_ref[lo:lo + t, :] = o.astype(o_ref.dtype)

    _two_stage(n, scratch, stage1, stage2, init)


def _fox_attention(qg, kv, e_bias):
    B, S, N = qg.shape
    W = N // 2
    d = B_HEAD_DIM
    H = W // d
    steps = B * H
    s1 = lambda off: pl.BlockSpec(
        (None, S, d), lambda n: (jnp.minimum(n, steps - 1) // H, 0, off + jnp.minimum(n, steps - 1) % H))
    s2 = lambda off: pl.BlockSpec(
        (None, S, d), lambda n: (jnp.maximum(n - 1, 0) // H, 0, off + jnp.maximum(n - 1, 0) % H))
    e_spec = pl.BlockSpec((None, S, e_bias.shape[2]), lambda n: (jnp.minimum(n, steps - 1) // H, 0, 0))
    return pl.pallas_call(
        functools.partial(_fox_attn_kernel, heads=H, steps=steps),
        grid=(steps + 1,),
        in_specs=[s1(0), s1(0), e_spec, s2(H), s2(H)],
        out_specs=s2(0),
        out_shape=jax.ShapeDtypeStruct((B, S, W), BF16),
        scratch_shapes=[pltpu.VMEM((S, 2 * d), BF16)] * 2 + _p_scratch(S, 1, False),
        compiler_params=_params("arbitrary"),
        name="fox_attn",
    )(qg, kv, e_bias, kv, qg)


def kernel(x, c, w_mod, b_mod, ln_g, ln_b, a_w_in, a_w_out, a_lam_q1, a_lam_k1, a_lam_q2, a_lam_k2,
           a_subln_g, kv_w_mod, kv_b_mod, kv_w, kv_b_f, b_w_in, b_w_out):
    B, S, D = x.shape
    a_heads = a_w_out.shape[1] // A_V_DIM
    b_width = b_w_out.shape[1]
    slopes = jnp.asarray([2.0 ** (-8.0 * (h + 1) / a_heads) for h in range(a_heads)], dtype=F32)

    mod = _modulation(c, w_mod, b_mod)
    mod_kv = _modulation(c, kv_w_mod[None], kv_b_mod[None])[0]
    mod_kv = mod_kv.reshape(B, 1, 2 * D)

    kv = e_bias = None
    for l in range(DEPTH):
        mod_l = mod[l].reshape(B, 1, 3 * D)
        if l < N_A:
            lam_init = 0.8 - 0.6 * math.exp(-0.3 * l)
            proj = _modproj(x, mod_l, 0, 1, a_w_in[l].astype(BF16))
            o = _diff_attention(proj, slopes, a_lam_q1[l], a_lam_k1[l], a_lam_q2[l], a_lam_k2[l],
                                a_subln_g[l], lam_init)
            w_out = a_w_out[l]
        else:
            if kv is None:
                kv = _modproj(x, mod_kv, 0, 1, kv_w[:, :2 * b_width].astype(BF16))
                b_heads = kv_b_f.shape[0]
                pad = B_HEAD_DIM - 3 * b_heads
                wf3 = jnp.pad(jnp.tile(kv_w[:, 2 * b_width:], (1, 3)), ((0, 0), (0, pad))).astype(BF16)
                bf3 = jnp.pad(jnp.tile(kv_b_f, 3), (0, pad)).reshape(1, B_HEAD_DIM)
                e_bias = _forget_gate_bias(x, mod_kv, wf3, bf3, b_heads)
            qg = _modproj(x, mod_l, 0, 1, b_w_in[l - N_A].astype(BF16))
            o = _fox_attention(qg, kv, e_bias)
            w_out = b_w_out[l - N_A]
        x = _outproj_ln(o, w_out.astype(BF16), x, mod_l, 2, ln_g[l], ln_b[l])
    return x
```

```python
import functools
import math

import jax
import jax.numpy as jnp
from jax import lax
from jax.experimental import pallas as pl
from jax.experimental.pallas import tpu as pltpu

DEPTH = 4
N_A = DEPTH // 2
A_QK_DIM = 128
A_V_DIM = 2 * A_QK_DIM
B_HEAD_DIM = 128
NEG = -1e30
EPS = 1e-5
DN_ALPHA = (2.0 * DEPTH) ** 0.25
LOG2E = math.log2(math.e)
BF16_EXACT_INT = 256

F32 = jnp.float32
BF16 = jnp.bfloat16

VMEM_LIMIT_BYTES = 56 * 1024 * 1024

MOD_TN = 1024
PROJ_TM = 1024
PROJ_TN = 1024
OUT_TM = 1024
OUT_SPLIT = 4
GATE_TM = 512
GATE_CHUNK = 256
ATT_T = 256


def _params(*sem):
    return pltpu.CompilerParams(dimension_semantics=sem, vmem_limit_bytes=VMEM_LIMIT_BYTES)


def _silu(x):
    return x / (1.0 + jnp.exp(-x))


def _mod_kernel(c_ref, w_ref, b_ref, o_ref):
    c_act = _silu(c_ref[...]).astype(BF16)
    o_ref[...] = jnp.dot(c_act, w_ref[...].astype(BF16), preferred_element_type=F32) + b_ref[...]


def _modulation(c, w, b):
    L, D, N = w.shape
    B = c.shape[0]
    return pl.pallas_call(
        _mod_kernel,
        grid=(L, N // MOD_TN),
        in_specs=[
            pl.BlockSpec((B, D), lambda l, j: (0, 0)),
            pl.BlockSpec((None, D, MOD_TN), lambda l, j: (l, 0, j)),
            pl.BlockSpec((None, 1, MOD_TN), lambda l, j: (l, 0, j)),
        ],
        out_specs=pl.BlockSpec((None, B, MOD_TN), lambda l, j: (l, 0, j)),
        out_shape=jax.ShapeDtypeStruct((L, B, N), F32),
        compiler_params=_params("parallel", "parallel"),
        name="modulation",
    )(c, w, b.reshape(L, 1, N))


def _modproj_kernel(x_ref, shift_ref, scale_ref, w_ref, o_ref):
    h = (x_ref[...] * (1.0 + scale_ref[...]) + shift_ref[...]).astype(BF16)
    o_ref[...] = jnp.dot(h, w_ref[...].astype(BF16), preferred_element_type=F32).astype(o_ref.dtype)


def _modproj(x, mod, shift_blk, scale_blk, w, layer, n_cols):
    B, S, D = x.shape
    return pl.pallas_call(
        _modproj_kernel,
        grid=(B, S // PROJ_TM, n_cols // PROJ_TN),
        in_specs=[
            pl.BlockSpec((None, PROJ_TM, D), lambda b, i, j: (b, i, 0)),
            pl.BlockSpec((None, 1, D), lambda b, i, j: (b, 0, shift_blk)),
            pl.BlockSpec((None, 1, D), lambda b, i, j: (b, 0, scale_blk)),
            pl.BlockSpec((None, D, PROJ_TN), lambda b, i, j: (layer, 0, j)),
        ],
        out_specs=pl.BlockSpec((None, PROJ_TM, PROJ_TN), lambda b, i, j: (b, i, j)),
        out_shape=jax.ShapeDtypeStruct((B, S, n_cols), BF16),
        compiler_params=_params("parallel", "parallel", "parallel"),
        name="modproj",
    )(x, mod, mod, w)


def _outproj_ln_kernel(o_ref, w_ref, x_ref, gate_ref, g_ref, b_ref, y_ref):
    rows = o_ref.shape[0] // OUT_SPLIT
    gate = gate_ref[...] * (1.0 / DN_ALPHA)
    for r in range(OUT_SPLIT):
        sl = slice(r * rows, (r + 1) * rows)
        y = jnp.dot(o_ref[sl, :], w_ref[...], preferred_element_type=F32)
        z = x_ref[sl, :] + gate * y
        mu = jnp.mean(z, axis=-1, keepdims=True)
        zc = z - mu
        var = jnp.mean(zc * zc, axis=-1, keepdims=True)
        y_ref[sl, :] = zc * lax.rsqrt(var + EPS / DN_ALPHA ** 2) * g_ref[...] + b_ref[...]


def _outproj_ln(o, w, x, mod, gate_blk, ln_g, ln_b):
    B, S, D = x.shape
    W = o.shape[2]
    return pl.pallas_call(
        _outproj_ln_kernel,
        grid=(B, S // OUT_TM),
        in_specs=[
            pl.BlockSpec((None, OUT_TM, W), lambda b, i: (b, i, 0)),
            pl.BlockSpec((W, D), lambda b, i: (0, 0), pipeline_mode=pl.Buffered(1)),
            pl.BlockSpec((None, OUT_TM, D), lambda b, i: (b, i, 0)),
            pl.BlockSpec((None, 1, D), lambda b, i: (b, 0, gate_blk)),
            pl.BlockSpec((1, D), lambda b, i: (0, 0)),
            pl.BlockSpec((1, D), lambda b, i: (0, 0)),
        ],
        out_specs=pl.BlockSpec((None, OUT_TM, D), lambda b, i: (b, i, 0)),
        out_shape=jax.ShapeDtypeStruct((B, S, D), F32),
        compiler_params=_params("parallel", "parallel"),
        name="outproj_ln",
    )(o, w, x, mod, ln_g.reshape(1, D), ln_b.reshape(1, D))


def _qk(q, k):
    return lax.dot_general(q, k, (((1,), (1,)), ((), ())), preferred_element_type=F32)


def _causal_mask(t):
    row = lax.broadcasted_iota(jnp.int32, (t, t), 0)
    col = lax.broadcasted_iota(jnp.int32, (t, t), 1)
    return row >= col


def _split3(x):
    hi = x.astype(BF16)
    r = x - hi.astype(F32)
    mid = r.astype(BF16)
    return hi, mid, (r - mid.astype(F32)).astype(BF16)


def _softmax_rows(q, k_ref, lo, causal, p_ref, l_ref):
    t = q.shape[0]
    s_d = jnp.where(causal, _qk(q, k_ref[lo:lo + t, :]), NEG)
    m = jnp.max(s_d, axis=-1, keepdims=True)
    if lo:
        s_f = _qk(q, k_ref[0:lo, :])
        m = jnp.maximum(m, jnp.max(s_f, axis=-1, keepdims=True))
    p_d = jnp.exp2(s_d - m)
    p_ref[:, lo:lo + t] = p_d.astype(BF16)
    if lo:
        p_f = jnp.exp2(s_f - m)
        p_ref[:, 0:lo] = p_f.astype(BF16)
    if l_ref is not None:
        l = jnp.sum(p_d, axis=-1, keepdims=True)
        if lo:
            l = l + jnp.sum(p_f, axis=-1, keepdims=True)
        l_ref[...] = l


def _p_scratch(seq, maps, sums):
    t = ATT_T
    shapes = []
    for _ in range(2 * maps):
        for i in range(seq // t):
            shapes.append(pltpu.VMEM((t, (i + 1) * t), BF16))
            if sums:
                shapes.append(pltpu.VMEM((t, 1), F32))
    return shapes


def _split_scratch(refs, seq, maps, sums):
    rows = seq // ATT_T
    it = iter(refs)
    return [[[(next(it), next(it) if sums else None) for _ in range(rows)] for _ in range(maps)]
            for _ in range(2)]


def _two_stage(n, scratch, stage1, stage2, init):
    @pl.when(n == 0)
    def _():
        init()
        for per_map in scratch[1]:
            for p_ref, l_ref in per_map:
                p_ref[...] = jnp.ones_like(p_ref)
                if l_ref is not None:
                    l_ref[...] = jnp.ones_like(l_ref)

    for parity in (0, 1):
        @pl.when(n % 2 == parity)
        def _():
            stage1(scratch[parity])
            stage2(scratch[1 - parity])


def _diff_attn_kernel(slopes_ref, lq1_ref, lk1_ref, lq2_ref, lk2_ref, sg_ref,
                      q_ref, k_ref, v_ref, g_ref, o_ref, *scratch, lam_init, heads, steps):
    t = ATT_T
    d = A_QK_DIM
    seq = q_ref.shape[0]
    n = pl.program_id(0)
    kaug_refs, scratch = scratch[:2], _split_scratch(scratch[2:], seq, 2, True)
    lane = lax.broadcasted_iota(jnp.int32, (1, d), 1)

    def init():
        pos = lax.broadcasted_iota(jnp.int32, (seq, d), 0)
        col = lax.broadcasted_iota(jnp.int32, (seq, d), 1)
        a = (pos // BF16_EXACT_INT).astype(F32)
        b = (pos % BF16_EXACT_INT).astype(F32)
        tile = jnp.where(col < 3, a, jnp.where(col < 6, b, 0.0)).astype(BF16)
        for kaug_ref in kaug_refs:
            kaug_ref[:, d:] = tile

    def stage1(sets):
        slope2 = slopes_ref[jnp.minimum(n, steps - 1) % heads] * LOG2E
        pieces = [x.astype(F32) for x in _split3(jnp.full((1, d), slope2, F32))]
        q_bias = jnp.zeros((1, d), F32)
        for j, x in enumerate(pieces):
            q_bias = jnp.where(lane == j, float(BF16_EXACT_INT) * x, jnp.where(lane == 3 + j, x, q_bias))
        q_bias = jnp.broadcast_to(q_bias.astype(BF16), (t, d))
        causal = _causal_mask(t)
        for c in range(2):
            kaug_refs[c][:, :d] = k_ref[:, c * d:(c + 1) * d]
        for i in range(seq // t):
            lo = i * t
            q = (q_ref[lo:lo + t, :].astype(F32) * (d ** -0.5 * LOG2E)).astype(BF16)
            for c in range(2):
                q_aug = jnp.concatenate([q[:, c * d:(c + 1) * d], q_bias], axis=1)
                _softmax_rows(q_aug, kaug_refs[c], lo, causal, *sets[c][i])

    def stage2(sets):
        lam = (jnp.exp(jnp.sum(lq1_ref[...] * lk1_ref[...], keepdims=True))
               - jnp.exp(jnp.sum(lq2_ref[...] * lk2_ref[...], keepdims=True)) + lam_init)
        sub_gain = sg_ref[...] * (1.0 - lam_init)
        for i in range(seq // t):
            lo = i * t
            v = v_ref[0:lo + t, :]
            a1, a2 = (jnp.dot(sets[c][i][0][...], v, preferred_element_type=F32) for c in range(2))
            o = a1 * (1.0 / sets[0][i][1][...]) - a2 * (lam / sets[1][i][1][...])
            o = o * lax.rsqrt(jnp.mean(o * o, axis=-1, keepdims=True) + EPS) * sub_gain
            o = o * _silu(g_ref[lo:lo + t, :].astype(F32))
            o_ref[lo:lo + t, :] = o.astype(o_ref.dtype)

    _two_stage(n, scratch, stage1, stage2, init)


def _diff_attention(proj, slopes, lq1, lk1, lq2, lk2, subln_g, lam_init):
    B, S, N = proj.shape
    W = N // 4
    H = W // A_V_DIM
    steps = B * H
    smem = pl.BlockSpec(memory_space=pltpu.SMEM)
    vec = pl.BlockSpec((1, A_QK_DIM), lambda n: (0, 0))
    s1 = lambda off: pl.BlockSpec(
        (None, S, A_V_DIM), lambda n: (jnp.minimum(n, steps - 1) // H, 0, off + jnp.minimum(n, steps - 1) % H))
    s2 = lambda off: pl.BlockSpec(
        (None, S, A_V_DIM), lambda n: (jnp.maximum(n - 1, 0) // H, 0, off + jnp.maximum(n - 1, 0) % H))
    return pl.pallas_call(
        functools.partial(_diff_attn_kernel, lam_init=lam_init, heads=H, steps=steps),
        grid=(steps + 1,),
        in_specs=[smem, vec, vec, vec, vec, pl.BlockSpec((1, A_V_DIM), lambda n: (0, 0)),
                  s1(0), s1(H), s2(2 * H), s2(3 * H)],
        out_specs=s2(0),
        out_shape=jax.ShapeDtypeStruct((B, S, W), BF16),
        scratch_shapes=[pltpu.VMEM((S, 2 * A_QK_DIM), BF16)] * 2 + _p_scratch(S, 2, True),
        compiler_params=_params("arbitrary"),
        name="diff_attn",
    )(slopes, lq1.reshape(1, -1), lk1.reshape(1, -1), lq2.reshape(1, -1), lk2.reshape(1, -1),
      subln_g.reshape(1, -1), proj, proj, proj, proj)


def _fgate_kernel(x_ref, shift_ref, scale_ref, wf_ref, bf_ref, e_ref, carry_ref, *, heads):
    @pl.when(pl.program_id(1) == 0)
    def _():
        carry_ref[...] = jnp.zeros_like(carry_ref)

    tm, lanes = e_ref.shape
    h = (x_ref[...] * (1.0 + scale_ref[...]) + shift_ref[...]).astype(BF16)
    u = jnp.dot(h, wf_ref[...], preferred_element_type=F32) + bf_ref[...]
    log_f = jnp.minimum(u, 0.0) - jnp.log1p(jnp.exp(-jnp.abs(u)))
    c = GATE_CHUNK
    row = lax.broadcasted_iota(jnp.int32, (c, c), 0)
    col = lax.broadcasted_iota(jnp.int32, (c, c), 1)
    lower = jnp.where(row >= col, 1.0, 0.0).astype(BF16)
    lane = lax.broadcasted_iota(jnp.int32, (c, lanes), 1)
    carry = carry_ref[...]
    for r in range(tm // c):
        f_cum = carry
        for piece in _split3(log_f[r * c:(r + 1) * c, :]):
            f_cum = f_cum + jnp.dot(lower, piece, preferred_element_type=F32)
        carry = f_cum[c - 1:c, :]
        hi, mid, lo = _split3(f_cum * (-LOG2E))
        e_ref[r * c:(r + 1) * c, :] = jnp.where(
            lane < heads, hi, jnp.where(lane < 2 * heads, mid,
                                        jnp.where(lane < 3 * heads, lo, jnp.zeros_like(hi))))
    carry_ref[...] = carry


def _forget_gate_bias(x, mod, wf3, bf3, heads):
    B, S, D = x.shape
    lanes = wf3.shape[1]
    return pl.pallas_call(
        functools.partial(_fgate_kernel, heads=heads),
        grid=(B, S // GATE_TM),
        in_specs=[
            pl.BlockSpec((None, GATE_TM, D), lambda b, i: (b, i, 0)),
            pl.BlockSpec((None, 1, D), lambda b, i: (b, 0, 0)),
            pl.BlockSpec((None, 1, D), lambda b, i: (b, 0, 1)),
            pl.BlockSpec((D, lanes), lambda b, i: (0, 0)),
            pl.BlockSpec((1, lanes), lambda b, i: (0, 0)),
        ],
        out_specs=pl.BlockSpec((None, GATE_TM, lanes), lambda b, i: (b, i, 0)),
        out_shape=jax.ShapeDtypeStruct((B, S, lanes), BF16),
        scratch_shapes=[pltpu.VMEM((1, lanes), F32)],
        compiler_params=_params("parallel", "arbitrary"),
        name="fgate_scan",
    )(x, mod, mod, wf3, bf3)


def _fox_attn_kernel(q_ref, k_ref, e_ref, v_ref, g_ref, o_ref, kaug_ref, vaug_ref, *scratch, heads, steps):
    t = ATT_T
    d = B_HEAD_DIM
    seq = q_ref.shape[0]
    n = pl.program_id(0)
    scratch = _split_scratch(scratch, seq, 1, False)

    def init():
        col = lax.broadcasted_iota(jnp.int32, (seq, d), 1)
        vaug_ref[:, d:] = jnp.where(col == 0, 1.0, 0.0).astype(BF16)

    def stage1(sets):
        h = jnp.minimum(n, steps - 1) % heads
        lane = lax.broadcasted_iota(jnp.int32, (1, d), 1)
        pick = jnp.where(lane == h, 1.0, jnp.where(lane == heads + h, 1.0,
                                                    jnp.where(lane == 2 * heads + h, 1.0, 0.0)))
        q_bias = jnp.broadcast_to(pick.astype(BF16), (t, d))
        causal = _causal_mask(t)
        kaug_ref[:, :d] = k_ref[...]
        kaug_ref[:, d:] = e_ref[...]
        for i in range(seq // t):
            lo = i * t
            q = (q_ref[lo:lo + t, :].astype(F32) * (d ** -0.5 * LOG2E)).astype(BF16)
            _softmax_rows(jnp.concatenate([q, q_bias], axis=1), kaug_ref, lo, causal, *sets[0][i])

    def stage2(sets):
        vaug_ref[:, :d] = v_ref[...]
        for i in range(seq // t):
            lo = i * t
            acc = jnp.dot(sets[0][i][0][...], vaug_ref[0:lo + t, :], preferred_element_type=F32)
            o = acc[:, :d] / acc[:, d:d + 1] * _silu(g_ref[lo:lo + t, :].astype(F32))
            o_ref[lo:lo + t, :] = o.astype(o_ref.dtype)

    _two_stage(n, scratch, stage1, stage2, init)


def _fox_attention(qg, kv, e_bias):
    B, S, N = qg.shape
    W = N // 2
    d = B_HEAD_DIM
    H = W // d
    steps = B * H
    s1 = lambda off: pl.BlockSpec(
        (None, S, d), lambda n: (jnp.minimum(n, steps - 1) // H, 0, off + jnp.minimum(n, steps - 1) % H))
    s2 = lambda off: pl.BlockSpec(
        (None, S, d), lambda n: (jnp.maximum(n - 1, 0) // H, 0, off + jnp.maximum(n - 1, 0) % H))
    e_spec = pl.BlockSpec((None, S, e_bias.shape[2]), lambda n: (jnp.minimum(n, steps - 1) // H, 0, 0))
    return pl.pallas_call(
        functools.partial(_fox_attn_kernel, heads=H, steps=steps),
        grid=(steps + 1,),
        in_specs=[s1(0), s1(0), e_spec, s2(H), s2(H)],
        out_specs=s2(0),
        out_shape=jax.ShapeDtypeStruct((B, S, W), BF16),
        scratch_shapes=[pltpu.VMEM((S, 2 * d), BF16)] * 2 + _p_scratch(S, 1, False),
        compiler_params=_params("arbitrary"),
        name="fox_attn",
    )(qg, kv, e_bias, kv, qg)


def kernel(x, c, w_mod, b_mod, ln_g, ln_b, a_w_in, a_w_out, a_lam_q1, a_lam_k1, a_lam_q2, a_lam_k2,
           a_subln_g, kv_w_mod, kv_b_mod, kv_w, kv_b_f, b_w_in, b_w_out):
    B, S, D = x.shape
    a_heads = a_w_out.shape[1] // A_V_DIM
    b_width = b_w_out.shape[1]
    slopes = jnp.asarray([2.0 ** (-8.0 * (h + 1) / a_heads) for h in range(a_heads)], dtype=F32)

    mod = _modulation(c, w_mod, b_mod)
    mod_kv = _modulation(c, kv_w_mod[None], kv_b_mod[None])[0]
    mod_kv = mod_kv.reshape(B, 1, 2 * D)

    kv = e_bias = None
    for l in range(DEPTH):
        mod_l = mod[l].reshape(B, 1, 3 * D)
        if l < N_A:
            lam_init = 0.8 - 0.6 * math.exp(-0.3 * l)
            proj = _modproj(x, mod_l, 0, 1, a_w_in, l, a_w_in.shape[2])
            o = _diff_attention(proj, slopes, a_lam_q1[l], a_lam_k1[l], a_lam_q2[l], a_lam_k2[l],
                                a_subln_g[l], lam_init)
            w_out = a_w_out[l]
        else:
            if kv is None:
                kv = _modproj(x, mod_kv, 0, 1, kv_w[None], 0, 2 * b_width)
                b_heads = kv_b_f.shape[0]
                pad = B_HEAD_DIM - 3 * b_heads
                wf3 = jnp.pad(jnp.tile(kv_w[:, 2 * b_width:], (1, 3)), ((0, 0), (0, pad))).astype(BF16)
                bf3 = jnp.pad(jnp.tile(kv_b_f, 3), (0, pad)).reshape(1, B_HEAD_DIM)
                e_bias = _forget_gate_bias(x, mod_kv, wf3, bf3, b_heads)
            qg = _modproj(x, mod_l, 0, 1, b_w_in, l - N_A, b_w_in.shape[2])
            o = _fox_attention(qg, kv, e_bias)
            w_out = b_w_out[l - N_A]
        x = _outproj_ln(o, w_out.astype(BF16), x, mod_l, 2, ln_g[l], ln_b[l])
    return x
```

```python
import functools
import math

import jax
import jax.numpy as jnp
from jax import lax
from jax.experimental import pallas as pl
from jax.experimental.pallas import tpu as pltpu

DEPTH = 4
N_A = DEPTH // 2
A_QK_DIM = 128
A_V_DIM = 2 * A_QK_DIM
B_HEAD_DIM = 128
NEG = -1e30
EPS = 1e-5
DN_ALPHA = (2.0 * DEPTH) ** 0.25
LOG2E = math.log2(math.e)
BF16_EXACT_INT = 256

F32 = jnp.float32
BF16 = jnp.bfloat16

VMEM_LIMIT_BYTES = 56 * 1024 * 1024

MOD_TN = 1024
PROJ_TM = 1024
PROJ_TN = 1024
OUT_TM = 1024
OUT_SPLIT = 4
GATE_TM = 512
GATE_CHUNK = 256
ATT_T = 256
FOX_GROUP = 2


def _params(*sem):
    return pltpu.CompilerParams(dimension_semantics=sem, vmem_limit_bytes=VMEM_LIMIT_BYTES)


def _silu(x):
    return x / (1.0 + jnp.exp(-x))


def _mod_kernel(c_ref, w_ref, b_ref, o_ref):
    c_act = _silu(c_ref[...]).astype(BF16)
    o_ref[...] = jnp.dot(c_act, w_ref[...].astype(BF16), preferred_element_type=F32) + b_ref[...]


def _modulation(c, w, b):
    L, D, N = w.shape
    B = c.shape[0]
    return pl.pallas_call(
        _mod_kernel,
        grid=(L, N // MOD_TN),
        in_specs=[
            pl.BlockSpec((B, D), lambda l, j: (0, 0)),
            pl.BlockSpec((None, D, MOD_TN), lambda l, j: (l, 0, j)),
            pl.BlockSpec((None, 1, MOD_TN), lambda l, j: (l, 0, j)),
        ],
        out_specs=pl.BlockSpec((None, B, MOD_TN), lambda l, j: (l, 0, j)),
        out_shape=jax.ShapeDtypeStruct((L, B, N), F32),
        compiler_params=_params("parallel", "parallel"),
        name="modulation",
    )(c, w, b.reshape(L, 1, N))


def _modproj_kernel(x_ref, shift_ref, scale_ref, w_ref, o_ref):
    h = (x_ref[...] * (1.0 + scale_ref[...]) + shift_ref[...]).astype(BF16)
    o_ref[...] = jnp.dot(h, w_ref[...], preferred_element_type=F32).astype(o_ref.dtype)


def _modproj(x, mod, shift_blk, scale_blk, w, layer, n_cols):
    B, S, D = x.shape
    return pl.pallas_call(
        _modproj_kernel,
        grid=(B, S // PROJ_TM, n_cols // PROJ_TN),
        in_specs=[
            pl.BlockSpec((None, PROJ_TM, D), lambda b, i, j: (b, i, 0)),
            pl.BlockSpec((None, 1, D), lambda b, i, j: (b, 0, shift_blk)),
            pl.BlockSpec((None, 1, D), lambda b, i, j: (b, 0, scale_blk)),
            pl.BlockSpec((None, D, PROJ_TN), lambda b, i, j: (layer, 0, j)),
        ],
        out_specs=pl.BlockSpec((None, PROJ_TM, PROJ_TN), lambda b, i, j: (b, i, j)),
        out_shape=jax.ShapeDtypeStruct((B, S, n_cols), BF16),
        compiler_params=_params("parallel", "parallel", "parallel"),
        name="modproj",
    )(x, mod, mod, w)


def _outproj_ln_kernel(o_ref, w_ref, x_ref, gate_ref, g_ref, b_ref, y_ref):
    rows = o_ref.shape[0] // OUT_SPLIT
    gate = gate_ref[...] * (1.0 / DN_ALPHA)
    for r in range(OUT_SPLIT):
        sl = slice(r * rows, (r + 1) * rows)
        y = jnp.dot(o_ref[sl, :], w_ref[...], preferred_element_type=F32)
        z = x_ref[sl, :] + gate * y
        mu = jnp.mean(z, axis=-1, keepdims=True)
        zc = z - mu
        var = jnp.mean(zc * zc, axis=-1, keepdims=True)
        y_ref[sl, :] = zc * lax.rsqrt(var + EPS / DN_ALPHA ** 2) * g_ref[...] + b_ref[...]


def _outproj_ln(o, w, x, mod, gate_blk, ln_g, ln_b):
    B, S, D = x.shape
    W = o.shape[2]
    return pl.pallas_call(
        _outproj_ln_kernel,
        grid=(B, S // OUT_TM),
        in_specs=[
            pl.BlockSpec((None, OUT_TM, W), lambda b, i: (b, i, 0)),
            pl.BlockSpec((W, D), lambda b, i: (0, 0), pipeline_mode=pl.Buffered(1)),
            pl.BlockSpec((None, OUT_TM, D), lambda b, i: (b, i, 0)),
            pl.BlockSpec((None, 1, D), lambda b, i: (b, 0, gate_blk)),
            pl.BlockSpec((1, D), lambda b, i: (0, 0)),
            pl.BlockSpec((1, D), lambda b, i: (0, 0)),
        ],
        out_specs=pl.BlockSpec((None, OUT_TM, D), lambda b, i: (b, i, 0)),
        out_shape=jax.ShapeDtypeStruct((B, S, D), F32),
        compiler_params=_params("parallel", "parallel"),
        name="outproj_ln",
    )(o, w, x, mod, ln_g.reshape(1, D), ln_b.reshape(1, D))


def _qk(q, k):
    return lax.dot_general(q, k, (((1,), (1,)), ((), ())), preferred_element_type=F32)


def _causal_mask(t):
    row = lax.broadcasted_iota(jnp.int32, (t, t), 0)
    col = lax.broadcasted_iota(jnp.int32, (t, t), 1)
    return row >= col


def _split3(x):
    hi = x.astype(BF16)
    r = x - hi.astype(F32)
    mid = r.astype(BF16)
    return hi, mid, (r - mid.astype(F32)).astype(BF16)


def _softmax_rows(q, k_ref, lo, causal, p_ref, l_ref):
    t = q.shape[0]
    s_d = jnp.where(causal, _qk(q, k_ref[lo:lo + t, :]), NEG)
    m = jnp.max(s_d, axis=-1, keepdims=True)
    if lo:
        s_f = _qk(q, k_ref[0:lo, :])
        m = jnp.maximum(m, jnp.max(s_f, axis=-1, keepdims=True))
    p_d = jnp.exp2(s_d - m)
    p_ref[:, lo:lo + t] = p_d.astype(BF16)
    if lo:
        p_f = jnp.exp2(s_f - m)
        p_ref[:, 0:lo] = p_f.astype(BF16)
    if l_ref is not None:
        l = jnp.sum(p_d, axis=-1, keepdims=True)
        if lo:
            l = l + jnp.sum(p_f, axis=-1, keepdims=True)
        l_ref[...] = l


def _p_scratch(seq, maps, sums):
    t = ATT_T
    shapes = []
    for _ in range(2 * maps):
        for i in range(seq // t):
            shapes.append(pltpu.VMEM((t, (i + 1) * t), BF16))
            if sums:
                shapes.append(pltpu.VMEM((t, 1), F32))
    return shapes


def _split_scratch(refs, seq, maps, sums):
    rows = seq // ATT_T
    it = iter(refs)
    return [[[(next(it), next(it) if sums else None) for _ in range(rows)] for _ in range(maps)]
            for _ in range(2)]


def _two_stage(n, scratch, stage1, stage2, init):
    @pl.when(n == 0)
    def _():
        init()
        for per_map in scratch[1]:
            for p_ref, l_ref in per_map:
                p_ref[...] = jnp.ones_like(p_ref)
                if l_ref is not None:
                    l_ref[...] = jnp.ones_like(l_ref)

    for parity in (0, 1):
        @pl.when(n % 2 == parity)
        def _():
            stage1(scratch[parity])
            stage2(scratch[1 - parity])


def _diff_attn_kernel(slopes_ref, lq1_ref, lk1_ref, lq2_ref, lk2_ref, sg_ref,
                      q_ref, k_ref, v_ref, g_ref, o_ref, *scratch, lam_init, heads, steps):
    t = ATT_T
    d = A_QK_DIM
    seq = q_ref.shape[0]
    n = pl.program_id(0)
    kaug_refs, scratch = scratch[:2], _split_scratch(scratch[2:], seq, 2, True)
    lane = lax.broadcasted_iota(jnp.int32, (1, d), 1)

    def init():
        pos = lax.broadcasted_iota(jnp.int32, (seq, d), 0)
        col = lax.broadcasted_iota(jnp.int32, (seq, d), 1)
        a = (pos // BF16_EXACT_INT).astype(F32)
        b = (pos % BF16_EXACT_INT).astype(F32)
        tile = jnp.where(col < 3, a, jnp.where(col < 6, b, 0.0)).astype(BF16)
        for kaug_ref in kaug_refs:
            kaug_ref[:, d:] = tile

    def stage1(sets):
        slope2 = slopes_ref[jnp.minimum(n, steps - 1) % heads] * LOG2E
        pieces = [x.astype(F32) for x in _split3(jnp.full((1, d), slope2, F32))]
        q_bias = jnp.zeros((1, d), F32)
        for j, x in enumerate(pieces):
            q_bias = jnp.where(lane == j, float(BF16_EXACT_INT) * x, jnp.where(lane == 3 + j, x, q_bias))
        q_bias = jnp.broadcast_to(q_bias.astype(BF16), (t, d))
        causal = _causal_mask(t)
        for c in range(2):
            kaug_refs[c][:, :d] = k_ref[:, c * d:(c + 1) * d]
        for i in range(seq // t):
            lo = i * t
            q = (q_ref[lo:lo + t, :].astype(F32) * (d ** -0.5 * LOG2E)).astype(BF16)
            for c in range(2):
                q_aug = jnp.concatenate([q[:, c * d:(c + 1) * d], q_bias], axis=1)
                _softmax_rows(q_aug, kaug_refs[c], lo, causal, *sets[c][i])

    def stage2(sets):
        lam = (jnp.exp(jnp.sum(lq1_ref[...] * lk1_ref[...], keepdims=True))
               - jnp.exp(jnp.sum(lq2_ref[...] * lk2_ref[...], keepdims=True)) + lam_init)
        sub_gain = sg_ref[...] * (1.0 - lam_init)
        for i in range(seq // t):
            lo = i * t
            v = v_ref[0:lo + t, :]
            a1, a2 = (jnp.dot(sets[c][i][0][...], v, preferred_element_type=F32) for c in range(2))
            o = a1 * (1.0 / sets[0][i][1][...]) - a2 * (lam / sets[1][i][1][...])
            o = o * lax.rsqrt(jnp.mean(o * o, axis=-1, keepdims=True) + EPS) * sub_gain
            o = o * _silu(g_ref[lo:lo + t, :].astype(F32))
            o_ref[lo:lo + t, :] = o.astype(o_ref.dtype)

    _two_stage(n, scratch, stage1, stage2, init)


def _diff_attention(proj, slopes, lq1, lk1, lq2, lk2, subln_g, lam_init):
    B, S, N = proj.shape
    W = N // 4
    H = W // A_V_DIM
    steps = B * H
    smem = pl.BlockSpec(memory_space=pltpu.SMEM)
    vec = pl.BlockSpec((1, A_QK_DIM), lambda n: (0, 0))
    s1 = lambda off: pl.BlockSpec(
        (None, S, A_V_DIM), lambda n: (jnp.minimum(n, steps - 1) // H, 0, off + jnp.minimum(n, steps - 1) % H))
    s2 = lambda off: pl.BlockSpec(
        (None, S, A_V_DIM), lambda n: (jnp.maximum(n - 1, 0) // H, 0, off + jnp.maximum(n - 1, 0) % H))
    return pl.pallas_call(
        functools.partial(_diff_attn_kernel, lam_init=lam_init, heads=H, steps=steps),
        grid=(steps + 1,),
        in_specs=[smem, vec, vec, vec, vec, pl.BlockSpec((1, A_V_DIM), lambda n: (0, 0)),
                  s1(0), s1(H), s2(2 * H), s2(3 * H)],
        out_specs=s2(0),
        out_shape=jax.ShapeDtypeStruct((B, S, W), BF16),
        scratch_shapes=[pltpu.VMEM((S, 2 * A_QK_DIM), BF16)] * 2 + _p_scratch(S, 2, True),
        compiler_params=_params("arbitrary"),
        name="diff_attn",
    )(slopes, lq1.reshape(1, -1), lk1.reshape(1, -1), lq2.reshape(1, -1), lk2.reshape(1, -1),
      subln_g.reshape(1, -1), proj, proj, proj, proj)


def _fgate_kernel(x_ref, shift_ref, scale_ref, wf_ref, bf_ref, e_ref, carry_ref, *, heads):
    @pl.when(pl.program_id(1) == 0)
    def _():
        carry_ref[...] = jnp.zeros_like(carry_ref)

    tm, lanes = e_ref.shape
    h = (x_ref[...] * (1.0 + scale_ref[...]) + shift_ref[...]).astype(BF16)
    u = jnp.dot(h, wf_ref[...], preferred_element_type=F32) + bf_ref[...]
    log_f = jnp.minimum(u, 0.0) - jnp.log1p(jnp.exp(-jnp.abs(u)))
    c = GATE_CHUNK
    row = lax.broadcasted_iota(jnp.int32, (c, c), 0)
    col = lax.broadcasted_iota(jnp.int32, (c, c), 1)
    lower = jnp.where(row >= col, 1.0, 0.0).astype(BF16)
    lane = lax.broadcasted_iota(jnp.int32, (c, lanes), 1)
    carry = carry_ref[...]
    for r in range(tm // c):
        f_cum = carry
        for piece in _split3(log_f[r * c:(r + 1) * c, :]):
            f_cum = f_cum + jnp.dot(lower, piece, preferred_element_type=F32)
        carry = f_cum[c - 1:c, :]
        hi, mid, lo = _split3(f_cum * (-LOG2E))
        e_ref[r * c:(r + 1) * c, :] = jnp.where(
            lane < heads, hi, jnp.where(lane < 2 * heads, mid,
                                        jnp.where(lane < 3 * heads, lo, jnp.zeros_like(hi))))
    carry_ref[...] = carry


def _forget_gate_bias(x, mod, wf3, bf3, heads):
    B, S, D = x.shape
    lanes = wf3.shape[1]
    return pl.pallas_call(
        functools.partial(_fgate_kernel, heads=heads),
        grid=(B, S // GATE_TM),
        in_specs=[
            pl.BlockSpec((None, GATE_TM, D), lambda b, i: (b, i, 0)),
            pl.BlockSpec((None, 1, D), lambda b, i: (b, 0, 0)),
            pl.BlockSpec((None, 1, D), lambda b, i: (b, 0, 1)),
            pl.BlockSpec((D, lanes), lambda b, i: (0, 0)),
            pl.BlockSpec((1, lanes), lambda b, i: (0, 0)),
        ],
        out_specs=pl.BlockSpec((None, GATE_TM, lanes), lambda b, i: (b, i, 0)),
        out_shape=jax.ShapeDtypeStruct((B, S, lanes), BF16),
        scratch_shapes=[pltpu.VMEM((1, lanes), F32)],
        compiler_params=_params("parallel", "arbitrary"),
        name="fgate_scan",
    )(x, mod, mod, wf3, bf3)


def _fox_attn_kernel(q_ref, k_ref, e_ref, v_ref, g_ref, o_ref, *scratch, heads, steps):
    t = ATT_T
    d = B_HEAD_DIM
    grp = FOX_GROUP
    seq = q_ref.shape[0]
    n = pl.program_id(0)
    kaug_refs, vaug_refs = scratch[:grp], scratch[grp:2 * grp]
    scratch = _split_scratch(scratch[2 * grp:], seq, grp, False)

    def init():
        col = lax.broadcasted_iota(jnp.int32, (seq, d), 1)
        for vaug_ref in vaug_refs:
            vaug_ref[:, d:] = jnp.where(col == 0, 1.0, 0.0).astype(BF16)

    def stage1(sets):
        first = (jnp.minimum(n, steps - 1) % (heads // grp)) * grp
        lane = lax.broadcasted_iota(jnp.int32, (1, d), 1)
        causal = _causal_mask(t)
        for hh in range(grp):
            h = first + hh
            pick = jnp.where(lane == h, 1.0, jnp.where(lane == heads + h, 1.0,
                                                        jnp.where(lane == 2 * heads + h, 1.0, 0.0)))
            q_bias = jnp.broadcast_to(pick.astype(BF16), (t, d))
            kaug_refs[hh][:, :d] = k_ref[:, hh * d:(hh + 1) * d]
            kaug_refs[hh][:, d:] = e_ref[...]
            for i in range(seq // t):
                lo = i * t
                q = (q_ref[lo:lo + t, hh * d:(hh + 1) * d].astype(F32) * (d ** -0.5 * LOG2E)).astype(BF16)
                _softmax_rows(jnp.concatenate([q, q_bias], axis=1), kaug_refs[hh], lo, causal, *sets[hh][i])

    def stage2(sets):
        for hh in range(grp):
            cols = slice(hh * d, (hh + 1) * d)
            vaug_refs[hh][:, :d] = v_ref[:, cols]
            for i in range(seq // t):
                lo = i * t
                acc = jnp.dot(sets[hh][i][0][...], vaug_refs[hh][0:lo + t, :], preferred_element_type=F32)
                o = acc[:, :d] / acc[:, d:d + 1] * _silu(g_ref[lo:lo + t, cols].astype(F32))
                o_ref[lo:lo + t, cols] = o.astype(o_ref.dtype)

    _two_stage(n, scratch, stage1, stage2, init)


def _fox_attention(qg, kv, e_bias):
    B, S, N = qg.shape
    W = N // 2
    d = B_HEAD_DIM
    H = W // d
    groups = H // FOX_GROUP
    steps = B * groups
    s1 = lambda off: pl.BlockSpec(
        (None, S, FOX_GROUP * d),
        lambda n: (jnp.minimum(n, steps - 1) // groups, 0, off + jnp.minimum(n, steps - 1) % groups))
    s2 = lambda off: pl.BlockSpec(
        (None, S, FOX_GROUP * d),
        lambda n: (jnp.maximum(n - 1, 0) // groups, 0, off + jnp.maximum(n - 1, 0) % groups))
    e_spec = pl.BlockSpec((None, S, e_bias.shape[2]), lambda n: (jnp.minimum(n, steps - 1) // groups, 0, 0))
    return pl.pallas_call(
        functools.partial(_fox_attn_kernel, heads=H, steps=steps),
        grid=(steps + 1,),
        in_specs=[s1(0), s1(0), e_spec, s2(groups), s2(groups)],
        out_specs=s2(0),
        out_shape=jax.ShapeDtypeStruct((B, S, W), BF16),
        scratch_shapes=[pltpu.VMEM((S, 2 * d), BF16)] * (2 * FOX_GROUP) + _p_scratch(S, FOX_GROUP, False),
        compiler_params=_params("arbitrary"),
        name="fox_attn",
    )(qg, kv, e_bias, kv, qg)


def kernel(x, c, w_mod, b_mod, ln_g, ln_b, a_w_in, a_w_out, a_lam_q1, a_lam_k1, a_lam_q2, a_lam_k2,
           a_subln_g, kv_w_mod, kv_b_mod, kv_w, kv_b_f, b_w_in, b_w_out):
    B, S, D = x.shape
    a_heads = a_w_out.shape[1] // A_V_DIM
    b_width = b_w_out.shape[1]
    slopes = jnp.asarray([2.0 ** (-8.0 * (h + 1) / a_heads) for h in range(a_heads)], dtype=F32)

    mod = _modulation(c, w_mod, b_mod)
    mod_kv = _modulation(c, kv_w_mod[None], kv_b_mod[None])[0]
    mod_kv = mod_kv.reshape(B, 1, 2 * D)

    a_w_in, b_w_in, kv_w16 = a_w_in.astype(BF16), b_w_in.astype(BF16), kv_w.astype(BF16)
    kv = e_bias = None
    for l in range(DEPTH):
        mod_l = mod[l].reshape(B, 1, 3 * D)
        if l < N_A:
            lam_init = 0.8 - 0.6 * math.exp(-0.3 * l)
            proj = _modproj(x, mod_l, 0, 1, a_w_in, l, a_w_in.shape[2])
            o = _diff_attention(proj, slopes, a_lam_q1[l], a_lam_k1[l], a_lam_q2[l], a_lam_k2[l],
                                a_subln_g[l], lam_init)
            w_out = a_w_out[l]
        else:
            if kv is None:
                kv = _modproj(x, mod_kv, 0, 1, kv_w16[None], 0, 2 * b_width)
                b_heads = kv_b_f.shape[0]
                pad = B_HEAD_DIM - 3 * b_heads
                wf3 = jnp.pad(jnp.tile(kv_w16[:, 2 * b_width:], (1, 3)), ((0, 0), (0, pad)))
                bf3 = jnp.pad(jnp.tile(kv_b_f, 3), (0, pad)).reshape(1, B_HEAD_DIM)
                e_bias = _forget_gate_bias(x, mod_kv, wf3, bf3, b_heads)
            qg = _modproj(x, mod_l, 0, 1, b_w_in, l - N_A, b_w_in.shape[2])
            o = _fox_attention(qg, kv, e_bias)
            w_out = b_w_out[l - N_A]
        x = _outproj_ln(o, w_out.astype(BF16), x, mod_l, 2, ln_g[l], ln_b[l])
    return x
```

```python
import functools
import math

import jax
import jax.numpy as jnp
from jax import lax
from jax.experimental import pallas as pl
from jax.experimental.pallas import tpu as pltpu

DEPTH = 4
N_A = DEPTH // 2
A_QK_DIM = 128
A_V_DIM = 2 * A_QK_DIM
B_HEAD_DIM = 128
NEG = -1e30
EPS = 1e-5
DN_ALPHA = (2.0 * DEPTH) ** 0.25
LOG2E = math.log2(math.e)
BF16_EXACT_INT = 256

F32 = jnp.float32
BF16 = jnp.bfloat16

VMEM_LIMIT_BYTES = 56 * 1024 * 1024

MOD_TN = 1024
PROJ_TM = 1024
PROJ_TN = 2048
OUT_TM = 1024
OUT_SPLIT = 4
GATE_TM = 1024
GATE_CHUNK = 256
ATT_T = 256
FOX_GROUP = 2


def _params(*sem):
    return pltpu.CompilerParams(dimension_semantics=sem, vmem_limit_bytes=VMEM_LIMIT_BYTES)


def _silu(x):
    return x / (1.0 + jnp.exp(-x))


def _mod_kernel(c_ref, w_ref, b_ref, o_ref):
    c_act = _silu(c_ref[...]).astype(BF16)
    o_ref[...] = jnp.dot(c_act, w_ref[...].astype(BF16), preferred_element_type=F32) + b_ref[...]


def _modulation(c, w, b):
    L, D, N = w.shape
    B = c.shape[0]
    return pl.pallas_call(
        _mod_kernel,
        grid=(L, N // MOD_TN),
        in_specs=[
            pl.BlockSpec((B, D), lambda l, j: (0, 0)),
            pl.BlockSpec((None, D, MOD_TN), lambda l, j: (l, 0, j)),
            pl.BlockSpec((None, 1, MOD_TN), lambda l, j: (l, 0, j)),
        ],
        out_specs=pl.BlockSpec((None, B, MOD_TN), lambda l, j: (l, 0, j)),
        out_shape=jax.ShapeDtypeStruct((L, B, N), F32),
        compiler_params=_params("parallel", "parallel"),
        name="modulation",
    )(c, w, b.reshape(L, 1, N))


def _modproj_kernel(x_ref, shift_ref, scale_ref, w_ref, o_ref):
    h = (x_ref[...] * (1.0 + scale_ref[...]) + shift_ref[...]).astype(BF16)
    o_ref[...] = jnp.dot(h, w_ref[...], preferred_element_type=F32).astype(o_ref.dtype)


def _modproj(x, mod, shift_blk, scale_blk, w, layer, n_cols):
    B, S, D = x.shape
    return pl.pallas_call(
        _modproj_kernel,
        grid=(B, S // PROJ_TM, n_cols // PROJ_TN),
        in_specs=[
            pl.BlockSpec((None, PROJ_TM, D), lambda b, i, j: (b, i, 0)),
            pl.BlockSpec((None, 1, D), lambda b, i, j: (b, 0, shift_blk)),
            pl.BlockSpec((None, 1, D), lambda b, i, j: (b, 0, scale_blk)),
            pl.BlockSpec((None, D, PROJ_TN), lambda b, i, j: (layer, 0, j)),
        ],
        out_specs=pl.BlockSpec((None, PROJ_TM, PROJ_TN), lambda b, i, j: (b, i, j)),
        out_shape=jax.ShapeDtypeStruct((B, S, n_cols), BF16),
        compiler_params=_params("parallel", "parallel", "parallel"),
        name="modproj",
    )(x, mod, mod, w)


def _outproj_ln_kernel(o_ref, w_ref, x_ref, gate_ref, g_ref, b_ref, y_ref):
    rows = o_ref.shape[0] // OUT_SPLIT
    gate = gate_ref[...] * (1.0 / DN_ALPHA)
    for r in range(OUT_SPLIT):
        sl = slice(r * rows, (r + 1) * rows)
        y = jnp.dot(o_ref[sl, :], w_ref[...], preferred_element_type=F32)
        z = x_ref[sl, :] + gate * y
        mu = jnp.mean(z, axis=-1, keepdims=True)
        zc = z - mu
        var = jnp.mean(zc * zc, axis=-1, keepdims=True)
        y_ref[sl, :] = zc * lax.rsqrt(var + EPS / DN_ALPHA ** 2) * g_ref[...] + b_ref[...]


def _outproj_ln(o, w, x, mod, gate_blk, ln_g, ln_b):
    B, S, D = x.shape
    W = o.shape[2]
    return pl.pallas_call(
        _outproj_ln_kernel,
        grid=(B, S // OUT_TM),
        in_specs=[
            pl.BlockSpec((None, OUT_TM, W), lambda b, i: (b, i, 0)),
            pl.BlockSpec((W, D), lambda b, i: (0, 0), pipeline_mode=pl.Buffered(1)),
            pl.BlockSpec((None, OUT_TM, D), lambda b, i: (b, i, 0)),
            pl.BlockSpec((None, 1, D), lambda b, i: (b, 0, gate_blk)),
            pl.BlockSpec((1, D), lambda b, i: (0, 0)),
            pl.BlockSpec((1, D), lambda b, i: (0, 0)),
        ],
        out_specs=pl.BlockSpec((None, OUT_TM, D), lambda b, i: (b, i, 0)),
        out_shape=jax.ShapeDtypeStruct((B, S, D), F32),
        compiler_params=_params("parallel", "parallel"),
        name="outproj_ln",
    )(o, w, x, mod, ln_g.reshape(1, D), ln_b.reshape(1, D))


def _qk(q, k):
    return lax.dot_general(q, k, (((1,), (1,)), ((), ())), preferred_element_type=F32)


def _causal_mask(t):
    row = lax.broadcasted_iota(jnp.int32, (t, t), 0)
    col = lax.broadcasted_iota(jnp.int32, (t, t), 1)
    return row >= col


def _split3(x):
    hi = x.astype(BF16)
    r = x - hi.astype(F32)
    mid = r.astype(BF16)
    return hi, mid, (r - mid.astype(F32)).astype(BF16)


def _softmax_rows(q, k_ref, lo, causal, p_ref, l_ref):
    t = q.shape[0]
    s_d = jnp.where(causal, _qk(q, k_ref[lo:lo + t, :]), NEG)
    m = jnp.max(s_d, axis=-1, keepdims=True)
    if lo:
        s_f = _qk(q, k_ref[0:lo, :])
        m = jnp.maximum(m, jnp.max(s_f, axis=-1, keepdims=True))
    p_d = jnp.exp2(s_d - m)
    p_ref[:, lo:lo + t] = p_d.astype(BF16)
    if lo:
        p_f = jnp.exp2(s_f - m)
        p_ref[:, 0:lo] = p_f.astype(BF16)
    if l_ref is not None:
        l = jnp.sum(p_d, axis=-1, keepdims=True)
        if lo:
            l = l + jnp.sum(p_f, axis=-1, keepdims=True)
        l_ref[...] = l


def _p_scratch(seq, maps, sums):
    t = ATT_T
    shapes = []
    for _ in range(2 * maps):
        for i in range(seq // t):
            shapes.append(pltpu.VMEM((t, (i + 1) * t), BF16))
            if sums:
                shapes.append(pltpu.VMEM((t, 1), F32))
    return shapes


def _split_scratch(refs, seq, maps, sums):
    rows = seq // ATT_T
    it = iter(refs)
    return [[[(next(it), next(it) if sums else None) for _ in range(rows)] for _ in range(maps)]
            for _ in range(2)]


def _two_stage(n, scratch, stage1, stage2, init):
    @pl.when(n == 0)
    def _():
        init()
        for per_map in scratch[1]:
            for p_ref, l_ref in per_map:
                p_ref[...] = jnp.ones_like(p_ref)
                if l_ref is not None:
                    l_ref[...] = jnp.ones_like(l_ref)

    for parity in (0, 1):
        @pl.when(n % 2 == parity)
        def _():
            stage1(scratch[parity])
            stage2(scratch[1 - parity])


def _diff_attn_kernel(slopes_ref, lq1_ref, lk1_ref, lq2_ref, lk2_ref, sg_ref,
                      q_ref, k_ref, v_ref, g_ref, o_ref, *scratch, lam_init, heads, steps):
    t = ATT_T
    d = A_QK_DIM
    seq = q_ref.shape[0]
    n = pl.program_id(0)
    kaug_refs, scratch = scratch[:2], _split_scratch(scratch[2:], seq, 2, True)
    lane = lax.broadcasted_iota(jnp.int32, (1, d), 1)

    def init():
        pos = lax.broadcasted_iota(jnp.int32, (seq, d), 0)
        col = lax.broadcasted_iota(jnp.int32, (seq, d), 1)
        a = (pos // BF16_EXACT_INT).astype(F32)
        b = (pos % BF16_EXACT_INT).astype(F32)
        tile = jnp.where(col < 3, a, jnp.where(col < 6, b, 0.0)).astype(BF16)
        for kaug_ref in kaug_refs:
            kaug_ref[:, d:] = tile

    def stage1(sets):
        slope2 = slopes_ref[jnp.minimum(n, steps - 1) % heads] * LOG2E
        pieces = [x.astype(F32) for x in _split3(jnp.full((1, d), slope2, F32))]
        q_bias = jnp.zeros((1, d), F32)
        for j, x in enumerate(pieces):
            q_bias = jnp.where(lane == j, float(BF16_EXACT_INT) * x, jnp.where(lane == 3 + j, x, q_bias))
        q_bias = jnp.broadcast_to(q_bias.astype(BF16), (t, d))
        causal = _causal_mask(t)
        for c in range(2):
            kaug_refs[c][:, :d] = k_ref[:, c * d:(c + 1) * d]
        for i in range(seq // t):
            lo = i * t
            q = (q_ref[lo:lo + t, :].astype(F32) * (d ** -0.5 * LOG2E)).astype(BF16)
            for c in range(2):
                q_aug = jnp.concatenate([q[:, c * d:(c + 1) * d], q_bias], axis=1)
                _softmax_rows(q_aug, kaug_refs[c], lo, causal, *sets[c][i])

    def stage2(sets):
        lam = (jnp.exp(jnp.sum(lq1_ref[...] * lk1_ref[...], keepdims=True))
               - jnp.exp(jnp.sum(lq2_ref[...] * lk2_ref[...], keepdims=True)) + lam_init)
        sub_gain = sg_ref[...] * (1.0 - lam_init)
        for i in range(seq // t):
            lo = i * t
            v = v_ref[0:lo + t, :]
            a1, a2 = (jnp.dot(sets[c][i][0][...], v, preferred_element_type=F32) for c in range(2))
            o = a1 * (1.0 / sets[0][i][1][...]) - a2 * (lam / sets[1][i][1][...])
            o = o * lax.rsqrt(jnp.mean(o * o, axis=-1, keepdims=True) + EPS) * sub_gain
            o = o * _silu(g_ref[lo:lo + t, :].astype(F32))
            o_ref[lo:lo + t, :] = o.astype(o_ref.dtype)

    _two_stage(n, scratch, stage1, stage2, init)


def _diff_attention(proj, slopes, lq1, lk1, lq2, lk2, subln_g, lam_init):
    B, S, N = proj.shape
    W = N // 4
    H = W // A_V_DIM
    steps = B * H
    smem = pl.BlockSpec(memory_space=pltpu.SMEM)
    vec = pl.BlockSpec((1, A_QK_DIM), lambda n: (0, 0))
    s1 = lambda off: pl.BlockSpec(
        (None, S, A_V_DIM), lambda n: (jnp.minimum(n, steps - 1) // H, 0, off + jnp.minimum(n, steps - 1) % H))
    s2 = lambda off: pl.BlockSpec(
        (None, S, A_V_DIM), lambda n: (jnp.maximum(n - 1, 0) // H, 0, off + jnp.maximum(n - 1, 0) % H))
    return pl.pallas_call(
        functools.partial(_diff_attn_kernel, lam_init=lam_init, heads=H, steps=steps),
        grid=(steps + 1,),
        in_specs=[smem, vec, vec, vec, vec, pl.BlockSpec((1, A_V_DIM), lambda n: (0, 0)),
                  s1(0), s1(H), s2(2 * H), s2(3 * H)],
        out_specs=s2(0),
        out_shape=jax.ShapeDtypeStruct((B, S, W), BF16),
        scratch_shapes=[pltpu.VMEM((S, 2 * A_QK_DIM), BF16)] * 2 + _p_scratch(S, 2, True),
        compiler_params=_params("arbitrary"),
        name="diff_attn",
    )(slopes, lq1.reshape(1, -1), lk1.reshape(1, -1), lq2.reshape(1, -1), lk2.reshape(1, -1),
      subln_g.reshape(1, -1), proj, proj, proj, proj)


def _fgate_kernel(x_ref, shift_ref, scale_ref, wf_ref, bf_ref, e_ref, carry_ref, *, heads):
    @pl.when(pl.program_id(1) == 0)
    def _():
        carry_ref[...] = jnp.zeros_like(carry_ref)

    tm, lanes = e_ref.shape
    h = (x_ref[...] * (1.0 + scale_ref[...]) + shift_ref[...]).astype(BF16)
    u = jnp.dot(h, wf_ref[...], preferred_element_type=F32) + bf_ref[...]
    log_f = jnp.minimum(u, 0.0) - jnp.log1p(jnp.exp(-jnp.abs(u)))
    c = GATE_CHUNK
    row = lax.broadcasted_iota(jnp.int32, (c, c), 0)
    col = lax.broadcasted_iota(jnp.int32, (c, c), 1)
    lower = jnp.where(row >= col, 1.0, 0.0).astype(BF16)
    lane = lax.broadcasted_iota(jnp.int32, (c, lanes), 1)
    carry = carry_ref[...]
    for r in range(tm // c):
        f_cum = carry
        for piece in _split3(log_f[r * c:(r + 1) * c, :]):
            f_cum = f_cum + jnp.dot(lower, piece, preferred_element_type=F32)
        carry = f_cum[c - 1:c, :]
        hi, mid, lo = _split3(f_cum * (-LOG2E))
        e_ref[r * c:(r + 1) * c, :] = jnp.where(
            lane < heads, hi, jnp.where(lane < 2 * heads, mid,
                                        jnp.where(lane < 3 * heads, lo, jnp.zeros_like(hi))))
    carry_ref[...] = carry


def _forget_gate_bias(x, mod, wf3, bf3, heads):
    B, S, D = x.shape
    lanes = wf3.shape[1]
    return pl.pallas_call(
        functools.partial(_fgate_kernel, heads=heads),
        grid=(B, S // GATE_TM),
        in_specs=[
            pl.BlockSpec((None, GATE_TM, D), lambda b, i: (b, i, 0)),
            pl.BlockSpec((None, 1, D), lambda b, i: (b, 0, 0)),
            pl.BlockSpec((None, 1, D), lambda b, i: (b, 0, 1)),
            pl.BlockSpec((D, lanes), lambda b, i: (0, 0)),
            pl.BlockSpec((1, lanes), lambda b, i: (0, 0)),
        ],
        out_specs=pl.BlockSpec((None, GATE_TM, lanes), lambda b, i: (b, i, 0)),
        out_shape=jax.ShapeDtypeStruct((B, S, lanes), BF16),
        scratch_shapes=[pltpu.VMEM((1, lanes), F32)],
        compiler_params=_params("parallel", "arbitrary"),
        name="fgate_scan",
    )(x, mod, mod, wf3, bf3)


def _fox_attn_kernel(q_ref, k_ref, e_ref, v_ref, g_ref, o_ref, *scratch, heads, steps):
    t = ATT_T
    d = B_HEAD_DIM
    grp = FOX_GROUP
    seq = q_ref.shape[0]
    n = pl.program_id(0)
    kaug_refs, vaug_refs = scratch[:grp], scratch[grp:2 * grp]
    scratch = _split_scratch(scratch[2 * grp:], seq, grp, False)

    def init():
        col = lax.broadcasted_iota(jnp.int32, (seq, d), 1)
        for vaug_ref in vaug_refs:
            vaug_ref[:, d:] = jnp.where(col == 0, 1.0, 0.0).astype(BF16)

    def stage1(sets):
        first = (jnp.minimum(n, steps - 1) % (heads // grp)) * grp
        lane = lax.broadcasted_iota(jnp.int32, (1, d), 1)
        causal = _causal_mask(t)
        for hh in range(grp):
            h = first + hh
            pick = jnp.where(lane == h, 1.0, jnp.where(lane == heads + h, 1.0,
                                                        jnp.where(lane == 2 * heads + h, 1.0, 0.0)))
            q_bias = jnp.broadcast_to(pick.astype(BF16), (t, d))
            kaug_refs[hh][:, :d] = k_ref[:, hh * d:(hh + 1) * d]
            kaug_refs[hh][:, d:] = e_ref[...]
            for i in range(seq // t):
                lo = i * t
                q = (q_ref[lo:lo + t, hh * d:(hh + 1) * d].astype(F32) * (d ** -0.5 * LOG2E)).astype(BF16)
                _softmax_rows(jnp.concatenate([q, q_bias], axis=1), kaug_refs[hh], lo, causal, *sets[hh][i])

    def stage2(sets):
        for hh in range(grp):
            cols = slice(hh * d, (hh + 1) * d)
            vaug_refs[hh][:, :d] = v_ref[:, cols]
            for i in range(seq // t):
                lo = i * t
                acc = jnp.dot(sets[hh][i][0][...], vaug_refs[hh][0:lo + t, :], preferred_element_type=F32)
                o = acc[:, :d] / acc[:, d:d + 1] * _silu(g_ref[lo:lo + t, cols].astype(F32))
                o_ref[lo:lo + t, cols] = o.astype(o_ref.dtype)

    _two_stage(n, scratch, stage1, stage2, init)


def _fox_attention(qg, kv, e_bias):
    B, S, N = qg.shape
    W = N // 2
    d = B_HEAD_DIM
    H = W // d
    groups = H // FOX_GROUP
    steps = B * groups
    s1 = lambda off: pl.BlockSpec(
        (None, S, FOX_GROUP * d),
        lambda n: (jnp.minimum(n, steps - 1) // groups, 0, off + jnp.minimum(n, steps - 1) % groups))
    s2 = lambda off: pl.BlockSpec(
        (None, S, FOX_GROUP * d),
        lambda n: (jnp.maximum(n - 1, 0) // groups, 0, off + jnp.maximum(n - 1, 0) % groups))
    e_spec = pl.BlockSpec((None, S, e_bias.shape[2]), lambda n: (jnp.minimum(n, steps - 1) // groups, 0, 0))
    return pl.pallas_call(
        functools.partial(_fox_attn_kernel, heads=H, steps=steps),
        grid=(steps + 1,),
        in_specs=[s1(0), s1(0), e_spec, s2(groups), s2(groups)],
        out_specs=s2(0),
        out_shape=jax.ShapeDtypeStruct((B, S, W), BF16),
        scratch_shapes=[pltpu.VMEM((S, 2 * d), BF16)] * (2 * FOX_GROUP) + _p_scratch(S, FOX_GROUP, False),
        compiler_params=_params("arbitrary"),
        name="fox_attn",
    )(qg, kv, e_bias, kv, qg)


def kernel(x, c, w_mod, b_mod, ln_g, ln_b, a_w_in, a_w_out, a_lam_q1, a_lam_k1, a_lam_q2, a_lam_k2,
           a_subln_g, kv_w_mod, kv_b_mod, kv_w, kv_b_f, b_w_in, b_w_out):
    B, S, D = x.shape
    a_heads = a_w_out.shape[1] // A_V_DIM
    b_width = b_w_out.shape[1]
    slopes = jnp.asarray([2.0 ** (-8.0 * (h + 1) / a_heads) for h in range(a_heads)], dtype=F32)

    mod = _modulation(c, w_mod, b_mod)
    mod_kv = _modulation(c, kv_w_mod[None], kv_b_mod[None])[0]
    mod_kv = mod_kv.reshape(B, 1, 2 * D)

    a_w_in, b_w_in, kv_w16 = a_w_in.astype(BF16), b_w_in.astype(BF16), kv_w.astype(BF16)
    kv = e_bias = None
    for l in range(DEPTH):
        mod_l = mod[l].reshape(B, 1, 3 * D)
        if l < N_A:
            lam_init = 0.8 - 0.6 * math.exp(-0.3 * l)
            proj = _modproj(x, mod_l, 0, 1, a_w_in, l, a_w_in.shape[2])
            o = _diff_attention(proj, slopes, a_lam_q1[l], a_lam_k1[l], a_lam_q2[l], a_lam_k2[l],
                                a_subln_g[l], lam_init)
            w_out = a_w_out[l]
        else:
            if kv is None:
                kv = _modproj(x, mod_kv, 0, 1, kv_w16[None], 0, 2 * b_width)
                b_heads = kv_b_f.shape[0]
                pad = B_HEAD_DIM - 3 * b_heads
                wf3 = jnp.pad(jnp.tile(kv_w16[:, 2 * b_width:], (1, 3)), ((0, 0), (0, pad)))
                bf3 = jnp.pad(jnp.tile(kv_b_f, 3), (0, pad)).reshape(1, B_HEAD_DIM)
                e_bias = _forget_gate_bias(x, mod_kv, wf3, bf3, b_heads)
            qg = _modproj(x, mod_l, 0, 1, b_w_in, l - N_A, b_w_in.shape[2])
            o = _fox_attention(qg, kv, e_bias)
            w_out = b_w_out[l - N_A]
        x = _outproj_ln(o, w_out.astype(BF16), x, mod_l, 2, ln_g[l], ln_b[l])
    return x
```

```python
import functools
import math

import jax
import jax.numpy as jnp
from jax import lax
from jax.experimental import pallas as pl
from jax.experimental.pallas import tpu as pltpu

DEPTH = 4
N_A = DEPTH // 2
A_QK_DIM = 128
A_V_DIM = 2 * A_QK_DIM
B_HEAD_DIM = 128
NEG = -1e30
EPS = 1e-5
DN_ALPHA = (2.0 * DEPTH) ** 0.25
LOG2E = math.log2(math.e)
BF16_EXACT_INT = 256

F32 = jnp.float32
BF16 = jnp.bfloat16

VMEM_LIMIT_BYTES = 56 * 1024 * 1024

MOD_TN = 1024
PROJ_TM = 1024
PROJ_TN = 2048
PROJ_RESIDENT_COLS = 4096
OUT_TM = 1024
OUT_SPLIT = 4
GATE_TM = 1024
GATE_CHUNK = 256
ATT_T = 256
FOX_GROUP = 2


def _params(*sem):
    return pltpu.CompilerParams(dimension_semantics=sem, vmem_limit_bytes=VMEM_LIMIT_BYTES)


def _silu(x):
    return x / (1.0 + jnp.exp(-x))


def _mod_kernel(c_ref, w_ref, b_ref, o_ref):
    c_act = _silu(c_ref[...]).astype(BF16)
    o_ref[...] = jnp.dot(c_act, w_ref[...].astype(BF16), preferred_element_type=F32) + b_ref[...]


def _modulation(c, w, b):
    L, D, N = w.shape
    B = c.shape[0]
    return pl.pallas_call(
        _mod_kernel,
        grid=(L, N // MOD_TN),
        in_specs=[
            pl.BlockSpec((B, D), lambda l, j: (0, 0)),
            pl.BlockSpec((None, D, MOD_TN), lambda l, j: (l, 0, j)),
            pl.BlockSpec((None, 1, MOD_TN), lambda l, j: (l, 0, j)),
        ],
        out_specs=pl.BlockSpec((None, B, MOD_TN), lambda l, j: (l, 0, j)),
        out_shape=jax.ShapeDtypeStruct((L, B, N), F32),
        compiler_params=_params("parallel", "parallel"),
        name="modulation",
    )(c, w, b.reshape(L, 1, N))


def _modproj_kernel(x_ref, shift_ref, scale_ref, w_ref, o_ref):
    h = (x_ref[...] * (1.0 + scale_ref[...]) + shift_ref[...]).astype(BF16)
    o_ref[...] = jnp.dot(h, w_ref[...], preferred_element_type=F32).astype(o_ref.dtype)


def _modproj(x, mod, shift_blk, scale_blk, w, layer, n_cols):
    B, S, D = x.shape
    resident = n_cols <= PROJ_RESIDENT_COLS
    tn = n_cols if resident else PROJ_TN
    tm = PROJ_TM * PROJ_TN // tn
    w_spec = pl.BlockSpec((None, D, tn), lambda b, i, j: (layer, 0, j),
                          pipeline_mode=pl.Buffered(1) if resident else None)
    return pl.pallas_call(
        _modproj_kernel,
        grid=(B, S // tm, n_cols // tn),
        in_specs=[
            pl.BlockSpec((None, tm, D), lambda b, i, j: (b, i, 0)),
            pl.BlockSpec((None, 1, D), lambda b, i, j: (b, 0, shift_blk)),
            pl.BlockSpec((None, 1, D), lambda b, i, j: (b, 0, scale_blk)),
            w_spec,
        ],
        out_specs=pl.BlockSpec((None, tm, tn), lambda b, i, j: (b, i, j)),
        out_shape=jax.ShapeDtypeStruct((B, S, n_cols), BF16),
        compiler_params=_params("parallel", "parallel", "parallel"),
        name="modproj",
    )(x, mod, mod, w)


def _outproj_ln_kernel(o_ref, w_ref, x_ref, gate_ref, g_ref, b_ref, y_ref):
    rows = o_ref.shape[0] // OUT_SPLIT
    gate = gate_ref[...] * (1.0 / DN_ALPHA)
    for r in range(OUT_SPLIT):
        sl = slice(r * rows, (r + 1) * rows)
        y = jnp.dot(o_ref[sl, :], w_ref[...], preferred_element_type=F32)
        z = x_ref[sl, :] + gate * y
        mu = jnp.mean(z, axis=-1, keepdims=True)
        zc = z - mu
        var = jnp.mean(zc * zc, axis=-1, keepdims=True)
        y_ref[sl, :] = zc * lax.rsqrt(var + EPS / DN_ALPHA ** 2) * g_ref[...] + b_ref[...]


def _outproj_ln(o, w, x, mod, gate_blk, ln_g, ln_b):
    B, S, D = x.shape
    W = o.shape[2]
    return pl.pallas_call(
        _outproj_ln_kernel,
        grid=(B, S // OUT_TM),
        in_specs=[
            pl.BlockSpec((None, OUT_TM, W), lambda b, i: (b, i, 0)),
            pl.BlockSpec((W, D), lambda b, i: (0, 0), pipeline_mode=pl.Buffered(1)),
            pl.BlockSpec((None, OUT_TM, D), lambda b, i: (b, i, 0)),
            pl.BlockSpec((None, 1, D), lambda b, i: (b, 0, gate_blk)),
            pl.BlockSpec((1, D), lambda b, i: (0, 0)),
            pl.BlockSpec((1, D), lambda b, i: (0, 0)),
        ],
        out_specs=pl.BlockSpec((None, OUT_TM, D), lambda b, i: (b, i, 0)),
        out_shape=jax.ShapeDtypeStruct((B, S, D), F32),
        compiler_params=_params("parallel", "parallel"),
        name="outproj_ln",
    )(o, w, x, mod, ln_g.reshape(1, D), ln_b.reshape(1, D))


def _qk(q, k):
    return lax.dot_general(q, k, (((1,), (1,)), ((), ())), preferred_element_type=F32)


def _causal_mask(t):
    row = lax.broadcasted_iota(jnp.int32, (t, t), 0)
    col = lax.broadcasted_iota(jnp.int32, (t, t), 1)
    return row >= col


def _split3(x):
    hi = x.astype(BF16)
    r = x - hi.astype(F32)
    mid = r.astype(BF16)
    return hi, mid, (r - mid.astype(F32)).astype(BF16)


def _softmax_rows(q, k_ref, lo, causal, p_ref, l_ref):
    t = q.shape[0]
    s_d = jnp.where(causal, _qk(q, k_ref[lo:lo + t, :]), NEG)
    m = jnp.max(s_d, axis=-1, keepdims=True)
    if lo:
        s_f = _qk(q, k_ref[0:lo, :])
        m = jnp.maximum(m, jnp.max(s_f, axis=-1, keepdims=True))
    p_d = jnp.exp2(s_d - m)
    p_ref[:, lo:lo + t] = p_d.astype(BF16)
    if lo:
        p_f = jnp.exp2(s_f - m)
        p_ref[:, 0:lo] = p_f.astype(BF16)
    if l_ref is not None:
        l = jnp.sum(p_d, axis=-1, keepdims=True)
        if lo:
            l = l + jnp.sum(p_f, axis=-1, keepdims=True)
        l_ref[...] = l


def _p_scratch(seq, maps, sums):
    t = ATT_T
    shapes = []
    for _ in range(2 * maps):
        for i in range(seq // t):
            shapes.append(pltpu.VMEM((t, (i + 1) * t), BF16))
            if sums:
                shapes.append(pltpu.VMEM((t, 1), F32))
    return shapes


def _split_scratch(refs, seq, maps, sums):
    rows = seq // ATT_T
    it = iter(refs)
    return [[[(next(it), next(it) if sums else None) for _ in range(rows)] for _ in range(maps)]
            for _ in range(2)]


def _two_stage(n, scratch, stage1, stage2, init):
    @pl.when(n == 0)
    def _():
        init()
        for per_map in scratch[1]:
            for p_ref, l_ref in per_map:
                p_ref[...] = jnp.ones_like(p_ref)
                if l_ref is not None:
                    l_ref[...] = jnp.ones_like(l_ref)

    for parity in (0, 1):
        @pl.when(n % 2 == parity)
        def _():
            stage1(scratch[parity])
            stage2(scratch[1 - parity])


def _diff_attn_kernel(slopes_ref, lq1_ref, lk1_ref, lq2_ref, lk2_ref, sg_ref,
                      q_ref, k_ref, v_ref, g_ref, o_ref, *scratch, lam_init, heads, steps):
    t = ATT_T
    d = A_QK_DIM
    seq = q_ref.shape[0]
    n = pl.program_id(0)
    kaug_refs, scratch = scratch[:2], _split_scratch(scratch[2:], seq, 2, True)
    lane = lax.broadcasted_iota(jnp.int32, (1, d), 1)

    def init():
        pos = lax.broadcasted_iota(jnp.int32, (seq, d), 0)
        col = lax.broadcasted_iota(jnp.int32, (seq, d), 1)
        a = (pos // BF16_EXACT_INT).astype(F32)
        b = (pos % BF16_EXACT_INT).astype(F32)
        tile = jnp.where(col < 3, a, jnp.where(col < 6, b, 0.0)).astype(BF16)
        for kaug_ref in kaug_refs:
            kaug_ref[:, d:] = tile

    def stage1(sets):
        slope2 = slopes_ref[jnp.minimum(n, steps - 1) % heads] * LOG2E
        pieces = [x.astype(F32) for x in _split3(jnp.full((1, d), slope2, F32))]
        q_bias = jnp.zeros((1, d), F32)
        for j, x in enumerate(pieces):
            q_bias = jnp.where(lane == j, float(BF16_EXACT_INT) * x, jnp.where(lane == 3 + j, x, q_bias))
        q_bias = jnp.broadcast_to(q_bias.astype(BF16), (t, d))
        causal = _causal_mask(t)
        for c in range(2):
            kaug_refs[c][:, :d] = k_ref[:, c * d:(c + 1) * d]
        for i in range(seq // t):
            lo = i * t
            q = (q_ref[lo:lo + t, :].astype(F32) * (d ** -0.5 * LOG2E)).astype(BF16)
            for c in range(2):
                q_aug = jnp.concatenate([q[:, c * d:(c + 1) * d], q_bias], axis=1)
                _softmax_rows(q_aug, kaug_refs[c], lo, causal, *sets[c][i])

    def stage2(sets):
        lam = (jnp.exp(jnp.sum(lq1_ref[...] * lk1_ref[...], keepdims=True))
               - jnp.exp(jnp.sum(lq2_ref[...] * lk2_ref[...], keepdims=True)) + lam_init)
        sub_gain = sg_ref[...] * (1.0 - lam_init)
        for i in range(seq // t):
            lo = i * t
            v = v_ref[0:lo + t, :]
            a1, a2 = (jnp.dot(sets[c][i][0][...], v, preferred_element_type=F32) for c in range(2))
            o = a1 * (1.0 / sets[0][i][1][...]) - a2 * (lam / sets[1][i][1][...])
            o = o * lax.rsqrt(jnp.mean(o * o, axis=-1, keepdims=True) + EPS) * sub_gain
            o = o * _silu(g_ref[lo:lo + t, :].astype(F32))
            o_ref[lo:lo + t, :] = o.astype(o_ref.dtype)

    _two_stage(n, scratch, stage1, stage2, init)


def _diff_attention(proj, slopes, lq1, lk1, lq2, lk2, subln_g, lam_init):
    B, S, N = proj.shape
    W = N // 4
    H = W // A_V_DIM
    steps = B * H
    smem = pl.BlockSpec(memory_space=pltpu.SMEM)
    vec = pl.BlockSpec((1, A_QK_DIM), lambda n: (0, 0))
    s1 = lambda off: pl.BlockSpec(
        (None, S, A_V_DIM), lambda n: (jnp.minimum(n, steps - 1) // H, 0, off + jnp.minimum(n, steps - 1) % H))
    s2 = lambda off: pl.BlockSpec(
        (None, S, A_V_DIM), lambda n: (jnp.maximum(n - 1, 0) // H, 0, off + jnp.maximum(n - 1, 0) % H))
    return pl.pallas_call(
        functools.partial(_diff_attn_kernel, lam_init=lam_init, heads=H, steps=steps),
        grid=(steps + 1,),
        in_specs=[smem, vec, vec, vec, vec, pl.BlockSpec((1, A_V_DIM), lambda n: (0, 0)),
                  s1(0), s1(H), s2(2 * H), s2(3 * H)],
        out_specs=s2(0),
        out_shape=jax.ShapeDtypeStruct((B, S, W), BF16),
        scratch_shapes=[pltpu.VMEM((S, 2 * A_QK_DIM), BF16)] * 2 + _p_scratch(S, 2, True),
        compiler_params=_params("arbitrary"),
        name="diff_attn",
    )(slopes, lq1.reshape(1, -1), lk1.reshape(1, -1), lq2.reshape(1, -1), lk2.reshape(1, -1),
      subln_g.reshape(1, -1), proj, proj, proj, proj)


def _fgate_kernel(x_ref, shift_ref, scale_ref, wf_ref, bf_ref, e_ref, carry_ref, *, heads):
    @pl.when(pl.program_id(1) == 0)
    def _():
        carry_ref[...] = jnp.zeros_like(carry_ref)

    tm, lanes = e_ref.shape
    h = (x_ref[...] * (1.0 + scale_ref[...]) + shift_ref[...]).astype(BF16)
    u = jnp.dot(h, wf_ref[...], preferred_element_type=F32) + bf_ref[...]
    log_f = jnp.minimum(u, 0.0) - jnp.log1p(jnp.exp(-jnp.abs(u)))
    c = GATE_CHUNK
    row = lax.broadcasted_iota(jnp.int32, (c, c), 0)
    col = lax.broadcasted_iota(jnp.int32, (c, c), 1)
    lower = jnp.where(row >= col, 1.0, 0.0).astype(BF16)
    lane = lax.broadcasted_iota(jnp.int32, (c, lanes), 1)
    carry = carry_ref[...]
    for r in range(tm // c):
        f_cum = carry
        for piece in _split3(log_f[r * c:(r + 1) * c, :]):
            f_cum = f_cum + jnp.dot(lower, piece, preferred_element_type=F32)
        carry = f_cum[c - 1:c, :]
        hi, mid, lo = _split3(f_cum * (-LOG2E))
        e_ref[r * c:(r + 1) * c, :] = jnp.where(
            lane < heads, hi, jnp.where(lane < 2 * heads, mid,
                                        jnp.where(lane < 3 * heads, lo, jnp.zeros_like(hi))))
    carry_ref[...] = carry


def _forget_gate_bias(x, mod, wf3, bf3, heads):
    B, S, D = x.shape
    lanes = wf3.shape[1]
    return pl.pallas_call(
        functools.partial(_fgate_kernel, heads=heads),
        grid=(B, S // GATE_TM),
        in_specs=[
            pl.BlockSpec((None, GATE_TM, D), lambda b, i: (b, i, 0)),
            pl.BlockSpec((None, 1, D), lambda b, i: (b, 0, 0)),
            pl.BlockSpec((None, 1, D), lambda b, i: (b, 0, 1)),
            pl.BlockSpec((D, lanes), lambda b, i: (0, 0)),
            pl.BlockSpec((1, lanes), lambda b, i: (0, 0)),
        ],
        out_specs=pl.BlockSpec((None, GATE_TM, lanes), lambda b, i: (b, i, 0)),
        out_shape=jax.ShapeDtypeStruct((B, S, lanes), BF16),
        scratch_shapes=[pltpu.VMEM((1, lanes), F32)],
        compiler_params=_params("parallel", "arbitrary"),
        name="fgate_scan",
    )(x, mod, mod, wf3, bf3)


def _fox_attn_kernel(q_ref, k_ref, e_ref, v_ref, g_ref, o_ref, *scratch, heads, steps):
    t = ATT_T
    d = B_HEAD_DIM
    grp = FOX_GROUP
    seq = q_ref.shape[0]
    n = pl.program_id(0)
    kaug_refs, vaug_refs = scratch[:grp], scratch[grp:2 * grp]
    scratch = _split_scratch(scratch[2 * grp:], seq, grp, False)

    def init():
        col = lax.broadcasted_iota(jnp.int32, (seq, d), 1)
        for vaug_ref in vaug_refs:
            vaug_ref[:, d:] = jnp.where(col == 0, 1.0, 0.0).astype(BF16)

    def stage1(sets):
        first = (jnp.minimum(n, steps - 1) % (heads // grp)) * grp
        lane = lax.broadcasted_iota(jnp.int32, (1, d), 1)
        causal = _causal_mask(t)
        for hh in range(grp):
            h = first + hh
            pick = jnp.where(lane == h, 1.0, jnp.where(lane == heads + h, 1.0,
                                                        jnp.where(lane == 2 * heads + h, 1.0, 0.0)))
            q_bias = jnp.broadcast_to(pick.astype(BF16), (t, d))
            kaug_refs[hh][:, :d] = k_ref[:, hh * d:(hh + 1) * d]
            kaug_refs[hh][:, d:] = e_ref[...]
            for i in range(seq // t):
                lo = i * t
                q = (q_ref[lo:lo + t, hh * d:(hh + 1) * d].astype(F32) * (d ** -0.5 * LOG2E)).astype(BF16)
                _softmax_rows(jnp.concatenate([q, q_bias], axis=1), kaug_refs[hh], lo, causal, *sets[hh][i])

    def stage2(sets):
        for hh in range(grp):
            cols = slice(hh * d, (hh + 1) * d)
            vaug_refs[hh][:, :d] = v_ref[:, cols]
            for i in range(seq // t):
                lo = i * t
                acc = jnp.dot(sets[hh][i][0][...], vaug_refs[hh][0:lo + t, :], preferred_element_type=F32)
                o = acc[:, :d] / acc[:, d:d + 1] * _silu(g_ref[lo:lo + t, cols].astype(F32))
                o_ref[lo:lo + t, cols] = o.astype(o_ref.dtype)

    _two_stage(n, scratch, stage1, stage2, init)


def _fox_attention(qg, kv, e_bias):
    B, S, N = qg.shape
    W = N // 2
    d = B_HEAD_DIM
    H = W // d
    groups = H // FOX_GROUP
    steps = B * groups
    s1 = lambda off: pl.BlockSpec(
        (None, S, FOX_GROUP * d),
        lambda n: (jnp.minimum(n, steps - 1) // groups, 0, off + jnp.minimum(n, steps - 1) % groups))
    s2 = lambda off: pl.BlockSpec(
        (None, S, FOX_GROUP * d),
        lambda n: (jnp.maximum(n - 1, 0) // groups, 0, off + jnp.maximum(n - 1, 0) % groups))
    e_spec = pl.BlockSpec((None, S, e_bias.shape[2]), lambda n: (jnp.minimum(n, steps - 1) // groups, 0, 0))
    return pl.pallas_call(
        functools.partial(_fox_attn_kernel, heads=H, steps=steps),
        grid=(steps + 1,),
        in_specs=[s1(0), s1(0), e_spec, s2(groups), s2(groups)],
        out_specs=s2(0),
        out_shape=jax.ShapeDtypeStruct((B, S, W), BF16),
        scratch_shapes=[pltpu.VMEM((S, 2 * d), BF16)] * (2 * FOX_GROUP) + _p_scratch(S, FOX_GROUP, False),
        compiler_params=_params("arbitrary"),
        name="fox_attn",
    )(qg, kv, e_bias, kv, qg)


def kernel(x, c, w_mod, b_mod, ln_g, ln_b, a_w_in, a_w_out, a_lam_q1, a_lam_k1, a_lam_q2, a_lam_k2,
           a_subln_g, kv_w_mod, kv_b_mod, kv_w, kv_b_f, b_w_in, b_w_out):
    B, S, D = x.shape
    a_heads = a_w_out.shape[1] // A_V_DIM
    b_width = b_w_out.shape[1]
    slopes = jnp.asarray([2.0 ** (-8.0 * (h + 1) / a_heads) for h in range(a_heads)], dtype=F32)

    mod = _modulation(c, w_mod, b_mod)
    mod_kv = _modulation(c, kv_w_mod[None], kv_b_mod[None])[0]
    mod_kv = mod_kv.reshape(B, 1, 2 * D)

    a_w_in, b_w_in, kv_w16 = a_w_in.astype(BF16), b_w_in.astype(BF16), kv_w.astype(BF16)
    kv = e_bias = None
    for l in range(DEPTH):
        mod_l = mod[l].reshape(B, 1, 3 * D)
        if l < N_A:
            lam_init = 0.8 - 0.6 * math.exp(-0.3 * l)
            proj = _modproj(x, mod_l, 0, 1, a_w_in, l, a_w_in.shape[2])
            o = _diff_attention(proj, slopes, a_lam_q1[l], a_lam_k1[l], a_lam_q2[l], a_lam_k2[l],
                                a_subln_g[l], lam_init)
            w_out = a_w_out[l]
        else:
            if kv is None:
                kv = _modproj(x, mod_kv, 0, 1, kv_w16[None], 0, 2 * b_width)
                b_heads = kv_b_f.shape[0]
                pad = B_HEAD_DIM - 3 * b_heads
                wf3 = jnp.pad(jnp.tile(kv_w16[:, 2 * b_width:], (1, 3)), ((0, 0), (0, pad)))
                bf3 = jnp.pad(jnp.tile(kv_b_f, 3), (0, pad)).reshape(1, B_HEAD_DIM)
                e_bias = _forget_gate_bias(x, mod_kv, wf3, bf3, b_heads)
            qg = _modproj(x, mod_l, 0, 1, b_w_in, l - N_A, b_w_in.shape[2])
            o = _fox_attention(qg, kv, e_bias)
            w_out = b_w_out[l - N_A]
        x = _outproj_ln(o, w_out.astype(BF16), x, mod_l, 2, ln_g[l], ln_b[l])
    return x
```

```python
import functools
import math

import jax
import jax.numpy as jnp
from jax import lax
from jax.experimental import pallas as pl
from jax.experimental.pallas import tpu as pltpu

DEPTH = 4
N_A = DEPTH // 2
A_QK_DIM = 128
A_V_DIM = 2 * A_QK_DIM
B_HEAD_DIM = 128
NEG = -1e30
EPS = 1e-5
DN_ALPHA = (2.0 * DEPTH) ** 0.25
LOG2E = math.log2(math.e)
BF16_EXACT_INT = 256

F32 = jnp.float32
BF16 = jnp.bfloat16

VMEM_LIMIT_BYTES = 56 * 1024 * 1024

MOD_TN = 1024
PROJ_TM = 1024
PROJ_TN = 2048
PROJ_RESIDENT_COLS = 4096
OUT_TM = 1024
OUT_SPLIT = 4
GATE_TM = 1024
GATE_CHUNK = 256
ATT_T = 256
FOX_GROUP = 2


def _params(*sem):
    return pltpu.CompilerParams(dimension_semantics=sem, vmem_limit_bytes=VMEM_LIMIT_BYTES)


def _silu(x):
    return x / (1.0 + jnp.exp(-x))


def _mod_kernel(c_ref, w_ref, b_ref, o_ref):
    c_act = _silu(c_ref[...]).astype(BF16)
    o_ref[...] = jnp.dot(c_act, w_ref[...].astype(BF16), preferred_element_type=F32) + b_ref[...]


def _modulation(c, w, b):
    L, D, N = w.shape
    B = c.shape[0]
    return pl.pallas_call(
        _mod_kernel,
        grid=(L, N // MOD_TN),
        in_specs=[
            pl.BlockSpec((B, D), lambda l, j: (0, 0)),
            pl.BlockSpec((None, D, MOD_TN), lambda l, j: (l, 0, j)),
            pl.BlockSpec((None, 1, MOD_TN), lambda l, j: (l, 0, j)),
        ],
        out_specs=pl.BlockSpec((None, B, MOD_TN), lambda l, j: (l, 0, j)),
        out_shape=jax.ShapeDtypeStruct((L, B, N), F32),
        compiler_params=_params("parallel", "parallel"),
        name="modulation",
    )(c, w, b.reshape(L, 1, N))


def _modproj_kernel(x_ref, shift_ref, scale_ref, w_ref, *rest):
    h = (x_ref[...] * (1.0 + scale_ref[...]) + shift_ref[...]).astype(BF16)
    if len(rest) == 1:
        (o_ref,) = rest
    else:
        wn_ref, bn_ref, o_ref, u_ref = rest
        u_ref[...] = jnp.dot(h, wn_ref[...], preferred_element_type=F32) + bn_ref[...]
    o_ref[...] = jnp.dot(h, w_ref[...], preferred_element_type=F32).astype(o_ref.dtype)


def _modproj(x, mod, shift_blk, scale_blk, w, layer, n_cols, narrow=None):
    B, S, D = x.shape
    resident = n_cols <= PROJ_RESIDENT_COLS
    tn = n_cols if resident else PROJ_TN
    tm = PROJ_TM * PROJ_TN // tn
    w_spec = pl.BlockSpec((None, D, tn), lambda b, i, j: (layer, 0, j),
                          pipeline_mode=pl.Buffered(1) if resident else None)
    in_specs = [
        pl.BlockSpec((None, tm, D), lambda b, i, j: (b, i, 0)),
        pl.BlockSpec((None, 1, D), lambda b, i, j: (b, 0, shift_blk)),
        pl.BlockSpec((None, 1, D), lambda b, i, j: (b, 0, scale_blk)),
        w_spec,
    ]
    out_specs = pl.BlockSpec((None, tm, tn), lambda b, i, j: (b, i, j))
    out_shape = jax.ShapeDtypeStruct((B, S, n_cols), BF16)
    args = (x, mod, mod, w)
    if narrow is not None:
        assert n_cols == tn, "the narrow projection needs a single column tile per row block"
        lanes = narrow[0].shape[1]
        in_specs += [pl.BlockSpec((D, lanes), lambda b, i, j: (0, 0)),
                     pl.BlockSpec((1, lanes), lambda b, i, j: (0, 0))]
        out_specs = (out_specs, pl.BlockSpec((None, tm, lanes), lambda b, i, j: (b, i, 0)))
        out_shape = (out_shape, jax.ShapeDtypeStruct((B, S, lanes), F32))
        args += tuple(narrow)
    return pl.pallas_call(
        _modproj_kernel,
        grid=(B, S // tm, n_cols // tn),
        in_specs=in_specs,
        out_specs=out_specs,
        out_shape=out_shape,
        compiler_params=_params("parallel", "parallel", "parallel"),
        name="modproj",
    )(*args)


def _outproj_ln_kernel(o_ref, w_ref, x_ref, gate_ref, g_ref, b_ref, y_ref):
    rows = o_ref.shape[0] // OUT_SPLIT
    gate = gate_ref[...] * (1.0 / DN_ALPHA)
    for r in range(OUT_SPLIT):
        sl = slice(r * rows, (r + 1) * rows)
        y = jnp.dot(o_ref[sl, :], w_ref[...], preferred_element_type=F32)
        z = x_ref[sl, :] + gate * y
        mu = jnp.mean(z, axis=-1, keepdims=True)
        zc = z - mu
        var = jnp.mean(zc * zc, axis=-1, keepdims=True)
        y_ref[sl, :] = zc * lax.rsqrt(var + EPS / DN_ALPHA ** 2) * g_ref[...] + b_ref[...]


def _outproj_ln(o, w, x, mod, gate_blk, ln_g, ln_b):
    B, S, D = x.shape
    W = o.shape[2]
    return pl.pallas_call(
        _outproj_ln_kernel,
        grid=(B, S // OUT_TM),
        in_specs=[
            pl.BlockSpec((None, OUT_TM, W), lambda b, i: (b, i, 0)),
            pl.BlockSpec((W, D), lambda b, i: (0, 0), pipeline_mode=pl.Buffered(1)),
            pl.BlockSpec((None, OUT_TM, D), lambda b, i: (b, i, 0)),
            pl.BlockSpec((None, 1, D), lambda b, i: (b, 0, gate_blk)),
            pl.BlockSpec((1, D), lambda b, i: (0, 0)),
            pl.BlockSpec((1, D), lambda b, i: (0, 0)),
        ],
        out_specs=pl.BlockSpec((None, OUT_TM, D), lambda b, i: (b, i, 0)),
        out_shape=jax.ShapeDtypeStruct((B, S, D), F32),
        compiler_params=_params("parallel", "parallel"),
        name="outproj_ln",
    )(o, w, x, mod, ln_g.reshape(1, D), ln_b.reshape(1, D))


def _qk(q, k):
    return lax.dot_general(q, k, (((1,), (1,)), ((), ())), preferred_element_type=F32)


def _causal_mask(t):
    row = lax.broadcasted_iota(jnp.int32, (t, t), 0)
    col = lax.broadcasted_iota(jnp.int32, (t, t), 1)
    return row >= col


def _split3(x):
    hi = x.astype(BF16)
    r = x - hi.astype(F32)
    mid = r.astype(BF16)
    return hi, mid, (r - mid.astype(F32)).astype(BF16)


def _softmax_rows(q, k_ref, lo, causal, p_ref, l_ref):
    t = q.shape[0]
    s_d = jnp.where(causal, _qk(q, k_ref[lo:lo + t, :]), NEG)
    m = jnp.max(s_d, axis=-1, keepdims=True)
    if lo:
        s_f = _qk(q, k_ref[0:lo, :])
        m = jnp.maximum(m, jnp.max(s_f, axis=-1, keepdims=True))
    p_d = jnp.exp2(s_d - m)
    p_ref[:, lo:lo + t] = p_d.astype(BF16)
    if lo:
        p_f = jnp.exp2(s_f - m)
        p_ref[:, 0:lo] = p_f.astype(BF16)
    if l_ref is not None:
        l = jnp.sum(p_d, axis=-1, keepdims=True)
        if lo:
            l = l + jnp.sum(p_f, axis=-1, keepdims=True)
        l_ref[...] = l


def _p_scratch(seq, maps, sums):
    t = ATT_T
    shapes = []
    for _ in range(2 * maps):
        for i in range(seq // t):
            shapes.append(pltpu.VMEM((t, (i + 1) * t), BF16))
            if sums:
                shapes.append(pltpu.VMEM((t, 1), F32))
    return shapes


def _split_scratch(refs, seq, maps, sums):
    rows = seq // ATT_T
    it = iter(refs)
    return [[[(next(it), next(it) if sums else None) for _ in range(rows)] for _ in range(maps)]
            for _ in range(2)]


def _two_stage(n, scratch, stage1, stage2, init):
    @pl.when(n == 0)
    def _():
        init()
        for per_map in scratch[1]:
            for p_ref, l_ref in per_map:
                p_ref[...] = jnp.ones_like(p_ref)
                if l_ref is not None:
                    l_ref[...] = jnp.ones_like(l_ref)

    for parity in (0, 1):
        @pl.when(n % 2 == parity)
        def _():
            stage1(scratch[parity])
            stage2(scratch[1 - parity])


def _diff_attn_kernel(slopes_ref, lq1_ref, lk1_ref, lq2_ref, lk2_ref, sg_ref,
                      q_ref, k_ref, v_ref, g_ref, o_ref, *scratch, lam_init, heads, steps):
    t = ATT_T
    d = A_QK_DIM
    seq = q_ref.shape[0]
    n = pl.program_id(0)
    kaug_refs, scratch = scratch[:2], _split_scratch(scratch[2:], seq, 2, True)
    lane = lax.broadcasted_iota(jnp.int32, (1, d), 1)

    def init():
        pos = lax.broadcasted_iota(jnp.int32, (seq, d), 0)
        col = lax.broadcasted_iota(jnp.int32, (seq, d), 1)
        a = (pos // BF16_EXACT_INT).astype(F32)
        b = (pos % BF16_EXACT_INT).astype(F32)
        tile = jnp.where(col < 3, a, jnp.where(col < 6, b, 0.0)).astype(BF16)
        for kaug_ref in kaug_refs:
            kaug_ref[:, d:] = tile

    def stage1(sets):
        slope2 = slopes_ref[jnp.minimum(n, steps - 1) % heads] * LOG2E
        pieces = [x.astype(F32) for x in _split3(jnp.full((1, d), slope2, F32))]
        q_bias = jnp.zeros((1, d), F32)
        for j, x in enumerate(pieces):
            q_bias = jnp.where(lane == j, float(BF16_EXACT_INT) * x, jnp.where(lane == 3 + j, x, q_bias))
        q_bias = jnp.broadcast_to(q_bias.astype(BF16), (t, d))
        causal = _causal_mask(t)
        for c in range(2):
            kaug_refs[c][:, :d] = k_ref[:, c * d:(c + 1) * d]
        for i in range(seq // t):
            lo = i * t
            q = (q_ref[lo:lo + t, :].astype(F32) * (d ** -0.5 * LOG2E)).astype(BF16)
            for c in range(2):
                q_aug = jnp.concatenate([q[:, c * d:(c + 1) * d], q_bias], axis=1)
                _softmax_rows(q_aug, kaug_refs[c], lo, causal, *sets[c][i])

    def stage2(sets):
        lam = (jnp.exp(jnp.sum(lq1_ref[...] * lk1_ref[...], keepdims=True))
               - jnp.exp(jnp.sum(lq2_ref[...] * lk2_ref[...], keepdims=True)) + lam_init)
        sub_gain = sg_ref[...] * (1.0 - lam_init)
        for i in range(seq // t):
            lo = i * t
            v = v_ref[0:lo + t, :]
            a1, a2 = (jnp.dot(sets[c][i][0][...], v, preferred_element_type=F32) for c in range(2))
            o = a1 * (1.0 / sets[0][i][1][...]) - a2 * (lam / sets[1][i][1][...])
            o = o * lax.rsqrt(jnp.mean(o * o, axis=-1, keepdims=True) + EPS) * sub_gain
            o = o * _silu(g_ref[lo:lo + t, :].astype(F32))
            o_ref[lo:lo + t, :] = o.astype(o_ref.dtype)

    _two_stage(n, scratch, stage1, stage2, init)


def _diff_attention(proj, slopes, lq1, lk1, lq2, lk2, subln_g, lam_init):
    B, S, N = proj.shape
    W = N // 4
    H = W // A_V_DIM
    steps = B * H
    smem = pl.BlockSpec(memory_space=pltpu.SMEM)
    vec = pl.BlockSpec((1, A_QK_DIM), lambda n: (0, 0))
    s1 = lambda off: pl.BlockSpec(
        (None, S, A_V_DIM), lambda n: (jnp.minimum(n, steps - 1) // H, 0, off + jnp.minimum(n, steps - 1) % H))
    s2 = lambda off: pl.BlockSpec(
        (None, S, A_V_DIM), lambda n: (jnp.maximum(n - 1, 0) // H, 0, off + jnp.maximum(n - 1, 0) % H))
    return pl.pallas_call(
        functools.partial(_diff_attn_kernel, lam_init=lam_init, heads=H, steps=steps),
        grid=(steps + 1,),
        in_specs=[smem, vec, vec, vec, vec, pl.BlockSpec((1, A_V_DIM), lambda n: (0, 0)),
                  s1(0), s1(H), s2(2 * H), s2(3 * H)],
        out_specs=s2(0),
        out_shape=jax.ShapeDtypeStruct((B, S, W), BF16),
        scratch_shapes=[pltpu.VMEM((S, 2 * A_QK_DIM), BF16)] * 2 + _p_scratch(S, 2, True),
        compiler_params=_params("arbitrary"),
        name="diff_attn",
    )(slopes, lq1.reshape(1, -1), lk1.reshape(1, -1), lq2.reshape(1, -1), lk2.reshape(1, -1),
      subln_g.reshape(1, -1), proj, proj, proj, proj)


def _fgate_kernel(u_ref, e_ref, carry_ref, *, heads):
    @pl.when(pl.program_id(1) == 0)
    def _():
        carry_ref[...] = jnp.zeros_like(carry_ref)

    tm, lanes = e_ref.shape
    u = u_ref[...]
    log_f = jnp.minimum(u, 0.0) - jnp.log1p(jnp.exp(-jnp.abs(u)))
    c = GATE_CHUNK
    row = lax.broadcasted_iota(jnp.int32, (c, c), 0)
    col = lax.broadcasted_iota(jnp.int32, (c, c), 1)
    lower = jnp.where(row >= col, 1.0, 0.0).astype(BF16)
    lane = lax.broadcasted_iota(jnp.int32, (c, lanes), 1)
    carry = carry_ref[...]
    for r in range(tm // c):
        f_cum = carry
        for piece in _split3(log_f[r * c:(r + 1) * c, :]):
            f_cum = f_cum + jnp.dot(lower, piece, preferred_element_type=F32)
        carry = f_cum[c - 1:c, :]
        hi, mid, lo = _split3(f_cum * (-LOG2E))
        e_ref[r * c:(r + 1) * c, :] = jnp.where(
            lane < heads, hi, jnp.where(lane < 2 * heads, mid,
                                        jnp.where(lane < 3 * heads, lo, jnp.zeros_like(hi))))
    carry_ref[...] = carry


def _forget_gate_bias(u, heads):
    B, S, lanes = u.shape
    return pl.pallas_call(
        functools.partial(_fgate_kernel, heads=heads),
        grid=(B, S // GATE_TM),
        in_specs=[pl.BlockSpec((None, GATE_TM, lanes), lambda b, i: (b, i, 0))],
        out_specs=pl.BlockSpec((None, GATE_TM, lanes), lambda b, i: (b, i, 0)),
        out_shape=jax.ShapeDtypeStruct((B, S, lanes), BF16),
        scratch_shapes=[pltpu.VMEM((1, lanes), F32)],
        compiler_params=_params("parallel", "arbitrary"),
        name="fgate_scan",
    )(u)


def _fox_attn_kernel(q_ref, k_ref, e_ref, v_ref, g_ref, o_ref, *scratch, heads, steps):
    t = ATT_T
    d = B_HEAD_DIM
    grp = FOX_GROUP
    seq = q_ref.shape[0]
    n = pl.program_id(0)
    kaug_refs, vaug_refs = scratch[:grp], scratch[grp:2 * grp]
    scratch = _split_scratch(scratch[2 * grp:], seq, grp, False)

    def init():
        col = lax.broadcasted_iota(jnp.int32, (seq, d), 1)
        for vaug_ref in vaug_refs:
            vaug_ref[:, d:] = jnp.where(col == 0, 1.0, 0.0).astype(BF16)

    def stage1(sets):
        first = (jnp.minimum(n, steps - 1) % (heads // grp)) * grp
        lane = lax.broadcasted_iota(jnp.int32, (1, d), 1)
        causal = _causal_mask(t)
        for hh in range(grp):
            h = first + hh
            pick = jnp.where(lane == h, 1.0, jnp.where(lane == heads + h, 1.0,
                                                        jnp.where(lane == 2 * heads + h, 1.0, 0.0)))
            q_bias = jnp.broadcast_to(pick.astype(BF16), (t, d))
            kaug_refs[hh][:, :d] = k_ref[:, hh * d:(hh + 1) * d]
            kaug_refs[hh][:, d:] = e_ref[...]
            for i in range(seq // t):
                lo = i * t
                q = (q_ref[lo:lo + t, hh * d:(hh + 1) * d].astype(F32) * (d ** -0.5 * LOG2E)).astype(BF16)
                _softmax_rows(jnp.concatenate([q, q_bias], axis=1), kaug_refs[hh], lo, causal, *sets[hh][i])

    def stage2(sets):
        for hh in range(grp):
            cols = slice(hh * d, (hh + 1) * d)
            vaug_refs[hh][:, :d] = v_ref[:, cols]
            for i in range(seq // t):
                lo = i * t
                acc = jnp.dot(sets[hh][i][0][...], vaug_refs[hh][0:lo + t, :], preferred_element_type=F32)
                o = acc[:, :d] / acc[:, d:d + 1] * _silu(g_ref[lo:lo + t, cols].astype(F32))
                o_ref[lo:lo + t, cols] = o.astype(o_ref.dtype)

    _two_stage(n, scratch, stage1, stage2, init)


def _fox_attention(qg, kv, e_bias):
    B, S, N = qg.shape
    W = N // 2
    d = B_HEAD_DIM
    H = W // d
    groups = H // FOX_GROUP
    steps = B * groups
    s1 = lambda off: pl.BlockSpec(
        (None, S, FOX_GROUP * d),
        lambda n: (jnp.minimum(n, steps - 1) // groups, 0, off + jnp.minimum(n, steps - 1) % groups))
    s2 = lambda off: pl.BlockSpec(
        (None, S, FOX_GROUP * d),
        lambda n: (jnp.maximum(n - 1, 0) // groups, 0, off + jnp.maximum(n - 1, 0) % groups))
    e_spec = pl.BlockSpec((None, S, e_bias.shape[2]), lambda n: (jnp.minimum(n, steps - 1) // groups, 0, 0))
    return pl.pallas_call(
        functools.partial(_fox_attn_kernel, heads=H, steps=steps),
        grid=(steps + 1,),
        in_specs=[s1(0), s1(0), e_spec, s2(groups), s2(groups)],
        out_specs=s2(0),
        out_shape=jax.ShapeDtypeStruct((B, S, W), BF16),
        scratch_shapes=[pltpu.VMEM((S, 2 * d), BF16)] * (2 * FOX_GROUP) + _p_scratch(S, FOX_GROUP, False),
        compiler_params=_params("arbitrary"),
        name="fox_attn",
    )(qg, kv, e_bias, kv, qg)


def kernel(x, c, w_mod, b_mod, ln_g, ln_b, a_w_in, a_w_out, a_lam_q1, a_lam_k1, a_lam_q2, a_lam_k2,
           a_subln_g, kv_w_mod, kv_b_mod, kv_w, kv_b_f, b_w_in, b_w_out):
    B, S, D = x.shape
    a_heads = a_w_out.shape[1] // A_V_DIM
    b_width = b_w_out.shape[1]
    slopes = jnp.asarray([2.0 ** (-8.0 * (h + 1) / a_heads) for h in range(a_heads)], dtype=F32)

    mod = _modulation(c, w_mod, b_mod)
    mod_kv = _modulation(c, kv_w_mod[None], kv_b_mod[None])[0]
    mod_kv = mod_kv.reshape(B, 1, 2 * D)

    a_w_in, b_w_in, kv_w16 = a_w_in.astype(BF16), b_w_in.astype(BF16), kv_w.astype(BF16)
    kv = e_bias = None
    for l in range(DEPTH):
        mod_l = mod[l].reshape(B, 1, 3 * D)
        if l < N_A:
            lam_init = 0.8 - 0.6 * math.exp(-0.3 * l)
            proj = _modproj(x, mod_l, 0, 1, a_w_in, l, a_w_in.shape[2])
            o = _diff_attention(proj, slopes, a_lam_q1[l], a_lam_k1[l], a_lam_q2[l], a_lam_k2[l],
                                a_subln_g[l], lam_init)
            w_out = a_w_out[l]
        else:
            if kv is None:
                b_heads = kv_b_f.shape[0]
                pad = B_HEAD_DIM - 3 * b_heads
                wf3 = jnp.pad(jnp.tile(kv_w16[:, 2 * b_width:], (1, 3)), ((0, 0), (0, pad)))
                bf3 = jnp.pad(jnp.tile(kv_b_f, 3), (0, pad)).reshape(1, B_HEAD_DIM)
                kv, gate_logits = _modproj(x, mod_kv, 0, 1, kv_w16[None], 0, 2 * b_width, (wf3, bf3))
                e_bias = _forget_gate_bias(gate_logits, b_heads)
            qg = _modproj(x, mod_l, 0, 1, b_w_in, l - N_A, b_w_in.shape[2])
            o = _fox_attention(qg, kv, e_bias)
            w_out = b_w_out[l - N_A]
        x = _outproj_ln(o, w_out.astype(BF16), x, mod_l, 2, ln_g[l], ln_b[l])
    return x
```
